```python
import jax, jax.numpy as jnp
from jax import lax
import numpy as np

D_MODEL = 2048
BATCH = 4
SEQ = 8192
DEPTH = 4
DEC_BATCH = 16
DEC_SEQ = 16
PAST_LEN = 2048

CHUNK = 64
Q_BLOCK = 128
K_BLOCK = 128
N_GROUPS = 4
GROUP_WIDTH = D_MODEL // N_GROUPS
A_HEADS = 8
A_HEAD_DIM = GROUP_WIDTH // A_HEADS
L_HEADS = 4
L_HEAD_DIM = GROUP_WIDTH // L_HEADS
BAND_CHUNKS = 8
A_WIN = BAND_CHUNKS * CHUNK
REL_CLIP = 128
N_REL = 2 * REL_CLIP + 1
Q_RANK = 384
KV_RANK = 256
NOPE_DIM = 128
ROPE_DIM = 64
V_DIM = GROUP_WIDTH // L_HEADS
ROPE_THETA = 10000.0
FORGET_BIAS_INIT = 3.0
D_FF = ((8 * D_MODEL + 3 * 256 - 1) // (3 * 256)) * 256
RMS_EPS = 1e-6
NEG_INF = -1e30
PAD_POS = 2 ** 30
A_SCALE = A_HEAD_DIM ** -0.5
L_SCALE = L_HEAD_DIM ** -0.5
MLA_SCALE = (NOPE_DIM + ROPE_DIM) ** -0.5
SPLIT_SIZES = (3 * GROUP_WIDTH, 3 * GROUP_WIDTH, L_HEADS, 3 * GROUP_WIDTH, Q_RANK, KV_RANK, ROPE_DIM)
SPLIT_POINTS = tuple(sum(SPLIT_SIZES[:i + 1]) for i in range(len(SPLIT_SIZES) - 1))
N_IN = sum(SPLIT_SIZES)

kernel_name = 'hybrid_streaming_encoder_step'


def rms_norm(x, g):
    xf = x.astype(jnp.float32)
    y = xf * lax.rsqrt(jnp.mean(xf * xf, axis=-1, keepdims=True) + RMS_EPS)
    return (y * g.astype(jnp.float32)).astype(x.dtype)


def rope(x, pos):
    half = ROPE_DIM // 2
    inv_freq = ROPE_THETA ** (-jnp.arange(half, dtype=jnp.float32) / half)
    ang = pos.astype(jnp.float32)[:, None] * inv_freq[None, :]
    cos = jnp.cos(ang)[None, :, None, :]
    sin = jnp.sin(ang)[None, :, None, :]
    xf = x.astype(jnp.float32)
    x1, x2 = xf[..., :half], xf[..., half:]
    return jnp.concatenate([x1 * cos - x2 * sin, x1 * sin + x2 * cos], axis=-1).astype(x.dtype)


def qkv_heads(z, n_heads, head_dim):
    B, S, _ = z.shape
    return [t.reshape(B, S, n_heads, head_dim) for t in jnp.split(z, 3, axis=-1)]


def softmax_attend(logits, v):
    m = jnp.max(logits, axis=-1, keepdims=True)
    p = jnp.exp(logits - m)
    denom = jnp.sum(p, axis=-1)
    o = jnp.einsum('bhqs,bshd->bqhd', p.astype(v.dtype), v)
    return o / jnp.moveaxis(denom, 1, 2)[..., None].astype(o.dtype)


def causal_sweep(block_fn, q_arrays, k_arrays, S):
    outs = []
    for i in range(S // Q_BLOCK):
        lo, hi = i * Q_BLOCK, (i + 1) * Q_BLOCK
        qs = [a[:, lo:hi] for a in q_arrays]
        ks = [a[:, :hi] for a in k_arrays]
        outs.append(block_fn(jnp.arange(lo, hi), *qs, *ks, jnp.arange(hi)))
    return jnp.concatenate(outs, axis=1)


def band_attention_prompt(q, k, v, rel_bias):
    B, S, H, Dh = q.shape
    n_chunks = S // CHUNK
    band = A_WIN + CHUNK
    pad = jnp.zeros((B, A_WIN, H, Dh), k.dtype)
    k_pad = jnp.concatenate([pad, k], axis=1)
    v_pad = jnp.concatenate([pad, v], axis=1)
    j = jnp.arange(band)
    rel = jnp.arange(CHUNK)[:, None] + A_WIN - j[None, :]
    bias = rel_bias[:, jnp.clip(rel, -REL_CLIP, REL_CLIP) + REL_CLIP].astype(jnp.float32)
    q_chunks = jnp.moveaxis(q.reshape(B, n_chunks, CHUNK, H, Dh), 1, 0)

    def one_chunk(args):
        c, qc = args
        start = c * CHUNK
        kb = lax.dynamic_slice_in_dim(k_pad, start, band, axis=1)
        vb = lax.dynamic_slice_in_dim(v_pad, start, band, axis=1)
        valid = (start - A_WIN + j) >= 0
        logits = jnp.einsum('bqhd,bkhd->bhqk', qc, kb).astype(jnp.float32) * A_SCALE + bias[None]
        logits = jnp.where(valid[None, None, None, :], logits, NEG_INF)
        return softmax_attend(logits, vb)

    out = lax.map(one_chunk, (jnp.arange(n_chunks), q_chunks))
    return jnp.moveaxis(out, 0, 1).reshape(B, S, H, Dh)


def band_attention_sample(q, k, v, cache_k, cache_v, rel_bias):
    T = q.shape[1]
    W = cache_k.shape[1]
    kk = jnp.concatenate([cache_k, k], axis=1)
    vv = jnp.concatenate([cache_v, v], axis=1)
    rel = (W + jnp.arange(T))[:, None] - jnp.arange(W + T)[None, :]
    bias = rel_bias[:, jnp.clip(rel, -REL_CLIP, REL_CLIP) + REL_CLIP].astype(jnp.float32)
    logits = jnp.einsum('bqhd,bkhd->bhqk', q, kk).astype(jnp.float32) * A_SCALE + bias[None]
    return softmax_attend(logits, vv)


def fox_block(qpos, q, q_off, k, v, k_off, kpos):
    logits = jnp.einsum('bqhd,bshd->bhqs', q, k).astype(jnp.float32) * L_SCALE
    logits = logits + jnp.moveaxis(q_off, 1, 2)[..., None] + jnp.moveaxis(k_off, 1, 2)[:, :, None, :]
    logits = jnp.where(kpos[None, :] <= qpos[:, None], logits, NEG_INF)
    return softmax_attend(logits, v)


def stick_breaking_block(qpos, q, k, v, kpos):
    L = k.shape[1]
    n_kb = -(-L // K_BLOCK)
    pad = n_kb * K_BLOCK - L
    if pad:
        B, _, H, Dh = k.shape
        k = jnp.concatenate([k, jnp.zeros((B, pad, H, Dh), k.dtype)], axis=1)
        v = jnp.concatenate([v, jnp.zeros((B, pad, H, v.shape[-1]), v.dtype)], axis=1)
        kpos = jnp.concatenate([kpos, jnp.full((pad,), PAD_POS, kpos.dtype)])
    z = jnp.einsum('bqhd,bshd->bhqs', q, k).astype(jnp.float32) * L_SCALE
    B, H, Q, Lp = z.shape
    mask = kpos[None, :] < qpos[:, None]
    log_beta = jax.nn.log_sigmoid(z)
    log_fail = jnp.where(mask, log_beta - z, 0.0)
    lf = log_fail.reshape(B, H, Q, n_kb, K_BLOCK)
    tri_in = jnp.asarray(np.tril(np.ones((K_BLOCK, K_BLOCK), np.float32), -1))
    tri_across = jnp.asarray(np.tril(np.ones((n_kb, n_kb), np.float32), -1))
    within = jnp.einsum('bhqnk,kj->bhqnj', lf, tri_in, precision=lax.Precision.HIGHEST)
    across = jnp.einsum('bhqn,nm->bhqm', jnp.sum(lf, axis=-1), tri_across, precision=lax.Precision.HIGHEST)
    log_surv = (within + across[..., None]).reshape(B, H, Q, Lp)
    w = jnp.where(mask, jnp.exp(log_beta + log_surv), 0.0)
    return jnp.einsum('bhqs,bshd->bqhd', w.astype(v.dtype), v)


def mla_block(qpos, q, k, v, kpos):
    logits = jnp.einsum('bqhd,bshd->bhqs', q, k).astype(jnp.float32) * MLA_SCALE
    mask = (kpos // CHUNK)[None, :] <= (qpos // CHUNK)[:, None]
    return softmax_attend(jnp.where(mask, logits, NEG_INF), v)


def mla_prep(dq, dkv, dkr, pos, g_q_lat, g_kv_lat, w_uq):
    B, S, _ = dq.shape
    qf = jnp.einsum('bsr,rn->bsn', rms_norm(dq, g_q_lat), w_uq).reshape(B, S, L_HEADS, NOPE_DIM + ROPE_DIM)
    q = jnp.concatenate([qf[..., :NOPE_DIM], rope(qf[..., NOPE_DIM:], pos)], axis=-1)
    ckv = rms_norm(dkv, g_kv_lat)
    kpe = rope(dkr[:, :, None, :], pos)[:, :, 0, :]
    return q, ckv, kpe


def mla_keys(ckv, kpe, w_uk, w_uv):
    B, S, _ = ckv.shape
    k_nope = jnp.einsum('bsr,rn->bsn', ckv, w_uk).reshape(B, S, L_HEADS, NOPE_DIM)
    k_rope = jnp.broadcast_to(kpe[:, :, None, :], (B, S, L_HEADS, ROPE_DIM))
    v = jnp.einsum('bsr,rn->bsn', ckv, w_uv).reshape(B, S, L_HEADS, V_DIM)
    return jnp.concatenate([k_nope, k_rope], axis=-1), v


def merge_groups(outs, g_mix):
    B, S = outs[0].shape[:2]
    return jnp.concatenate([rms_norm(o.reshape(B, S, GROUP_WIDTH), g_mix[i]) for i, o in enumerate(outs)], axis=-1)


def mixer_prompt(h, w_in, b_f, rel_bias, g_q_lat, g_kv_lat, w_uq, w_uk, w_uv, g_mix):
    B, S, _ = h.shape
    pos = jnp.arange(S)
    za, zb, zf, zc, dq, dkv, dkr = jnp.split(jnp.einsum('bsd,dn->bsn', h, w_in), SPLIT_POINTS, axis=-1)
    qa, ka, va = qkv_heads(za, A_HEADS, A_HEAD_DIM)
    oa = band_attention_prompt(qa, ka, va, rel_bias)
    qb, kb, vb = qkv_heads(zb, L_HEADS, L_HEAD_DIM)
    logf = jax.nn.log_sigmoid(zf.astype(jnp.float32) + b_f.astype(jnp.float32))
    F = jnp.cumsum(logf, axis=1)
    ob = causal_sweep(fox_block, (qb, F), (kb, vb, -F), S)
    qc, kc, vc = qkv_heads(zc, L_HEADS, L_HEAD_DIM)
    oc = causal_sweep(stick_breaking_block, (qc,), (kc, vc), S)
    qd, ckv, kpe = mla_prep(dq, dkv, dkr, pos, g_q_lat, g_kv_lat, w_uq)
    kd, vd = mla_keys(ckv, kpe, w_uk, w_uv)
    od = causal_sweep(mla_block, (qd,), (kd, vd), S)
    keep = min(A_WIN, S)
    state = (ka[:, S - keep:], va[:, S - keep:], kb, vb, logf, kc, vc, ckv, kpe)
    return merge_groups((oa, ob, oc, od), g_mix), state


def mixer_sample(h, ca_k, ca_v, cb_k, cb_v, cb_logf, cc_k, cc_v, cd_ckv, cd_kpe,
                 w_in, b_f, rel_bias, g_q_lat, g_kv_lat, w_uq, w_uk, w_uv, g_mix):
    B, T, _ = h.shape
    P = cb_k.shape[1]
    pos = P + jnp.arange(T)
    kpos = jnp.arange(P + T)
    za, zb, zf, zc, dq, dkv, dkr = jnp.split(jnp.einsum('bsd,dn->bsn', h, w_in), SPLIT_POINTS, axis=-1)
    qa, ka, va = qkv_heads(za, A_HEADS, A_HEAD_DIM)
    oa = band_attention_sample(qa, ka, va, ca_k, ca_v, rel_bias)
    qb, kb, vb = qkv_heads(zb, L_HEADS, L_HEAD_DIM)
    logf = jax.nn.log_sigmoid(zf.astype(jnp.float32) + b_f.astype(jnp.float32))
    Fn = jnp.cumsum(logf, axis=1)
    cl = cb_logf.astype(jnp.float32)
    suffix = jnp.cumsum(cl[:, ::-1], axis=1)[:, ::-1] - cl
    k_off = jnp.concatenate([suffix, -Fn], axis=1)
    ob = fox_block(pos, qb, Fn, jnp.concatenate([cb_k, kb], axis=1), jnp.concatenate([cb_v, vb], axis=1), k_off, kpos)
    qc, kc, vc = qkv_heads(zc, L_HEADS, L_HEAD_DIM)
    oc = stick_breaking_block(pos, qc, jnp.concatenate([cc_k, kc], axis=1), jnp.concatenate([cc_v, vc], axis=1), kpos)
    qd, ckv, kpe = mla_prep(dq, dkv, dkr, pos, g_q_lat, g_kv_lat, w_uq)
    kd, vd = mla_keys(jnp.concatenate([cd_ckv, ckv], axis=1), jnp.concatenate([cd_kpe, kpe], axis=1), w_uk, w_uv)
    od = mla_block(pos, qd, kd, vd, kpos)
    state = (ka, va, kb, vb, logf, kc, vc, ckv, kpe)
    return merge_groups((oa, ob, oc, od), g_mix), state


def residual_update(x, mixed, w_o, g_post_attn, g_pre_ffn, g_post_ffn, w_gate, w_up, w_down):
    x = x + rms_norm(jnp.einsum('bsd,de->bse', mixed, w_o), g_post_attn)
    h = rms_norm(x, g_pre_ffn)
    ff = jax.nn.silu(jnp.einsum('bsd,df->bsf', h, w_gate)) * jnp.einsum('bsd,df->bsf', h, w_up)
    return x + rms_norm(jnp.einsum('bsf,fd->bsd', ff, w_down), g_post_ffn)


def setup_inputs(seed: int = 0) -> dict:
    key = jax.random.key(seed)
    ks = jax.random.split(key, 32)

    def nrm(k, shape, scale=1.0):
        return scale * jax.random.normal(k, shape, jnp.float32)

    def gain(k, shape):
        return 1.0 + 0.05 * jax.random.normal(k, shape, jnp.float32)

    a_keep = min(A_WIN, PAST_LEN)
    ahd = (A_HEADS, A_HEAD_DIM)
    lhd = (L_HEADS, L_HEAD_DIM)
    return {
        'x_prompt': nrm(ks[0], (BATCH, SEQ, D_MODEL)),
        'x_sample': nrm(ks[1], (DEC_BATCH, DEC_SEQ, D_MODEL)),
        'cache_a_k': nrm(ks[2], (DEPTH, DEC_BATCH, a_keep) + ahd),
        'cache_a_v': nrm(ks[3], (DEPTH, DEC_BATCH, a_keep) + ahd),
        'cache_b_k': nrm(ks[4], (DEPTH, DEC_BATCH, PAST_LEN) + lhd),
        'cache_b_v': nrm(ks[5], (DEPTH, DEC_BATCH, PAST_LEN) + lhd),
        'cache_b_logf': jax.nn.log_sigmoid(FORGET_BIAS_INIT + nrm(ks[6], (DEPTH, DEC_BATCH, PAST_LEN, L_HEADS))),
        'cache_c_k': nrm(ks[7], (DEPTH, DEC_BATCH, PAST_LEN) + lhd),
        'cache_c_v': nrm(ks[8], (DEPTH, DEC_BATCH, PAST_LEN) + lhd),
        'cache_d_ckv': nrm(ks[9], (DEPTH, DEC_BATCH, PAST_LEN, KV_RANK)),
        'cache_d_kpe': nrm(ks[10], (DEPTH, DEC_BATCH, PAST_LEN, ROPE_DIM)),
        'w_in': nrm(ks[11], (DEPTH, D_MODEL, N_IN), D_MODEL ** -0.5),
        'b_f': FORGET_BIAS_INIT + nrm(ks[12], (DEPTH, L_HEADS), 0.1),
        'rel_bias': nrm(ks[13], (DEPTH, A_HEADS, N_REL), 0.5),
        'g_q_lat': gain(ks[14], (DEPTH, Q_RANK)),
        'g_kv_lat': gain(ks[15], (DEPTH, KV_RANK)),
        'w_uq': nrm(ks[16], (DEPTH, Q_RANK, L_HEADS * (NOPE_DIM + ROPE_DIM)), Q_RANK ** -0.5),
        'w_uk': nrm(ks[17], (DEPTH, KV_RANK, L_HEADS * NOPE_DIM), KV_RANK ** -0.5),
        'w_uv': nrm(ks[18], (DEPTH, KV_RANK, L_HEADS * V_DIM), KV_RANK ** -0.5),
        'g_mix': gain(ks[19], (DEPTH, N_GROUPS, GROUP_WIDTH)),
        'w_o': nrm(ks[20], (DEPTH, D_MODEL, D_MODEL), D_MODEL ** -0.5),
        'g_pre_attn': gain(ks[21], (DEPTH, D_MODEL)),
        'g_post_attn': gain(ks[22], (DEPTH, D_MODEL)),
        'g_pre_ffn': gain(ks[23], (DEPTH, D_MODEL)),
        'g_post_ffn': gain(ks[24], (DEPTH, D_MODEL)),
        'w_gate': nrm(ks[25], (DEPTH, D_MODEL, D_FF), D_MODEL ** -0.5),
        'w_up': nrm(ks[26], (DEPTH, D_MODEL, D_FF), D_MODEL ** -0.5),
        'w_down': nrm(ks[27], (DEPTH, D_FF, D_MODEL), D_FF ** -0.5),
    }


def reference(x_prompt, x_sample, cache_a_k, cache_a_v, cache_b_k, cache_b_v, cache_b_logf,
              cache_c_k, cache_c_v, cache_d_ckv, cache_d_kpe,
              w_in, b_f, rel_bias, g_q_lat, g_kv_lat, w_uq, w_uk, w_uv, g_mix, w_o,
              g_pre_attn, g_post_attn, g_pre_ffn, g_post_ffn, w_gate, w_up, w_down):
    xp, xs = x_prompt, x_sample
    p_states, s_states = [], []
    for l in range(DEPTH):
        lw = (w_in[l], b_f[l], rel_bias[l], g_q_lat[l], g_kv_lat[l], w_uq[l], w_uk[l], w_uv[l], g_mix[l])
        fw = (w_o[l], g_post_attn[l], g_pre_ffn[l], g_post_ffn[l], w_gate[l], w_up[l], w_down[l])
        mp, sp = mixer_prompt(rms_norm(xp, g_pre_attn[l]), *lw)
        ms, ss = mixer_sample(rms_norm(xs, g_pre_attn[l]), cache_a_k[l], cache_a_v[l], cache_b_k[l], cache_b_v[l],
                              cache_b_logf[l], cache_c_k[l], cache_c_v[l], cache_d_ckv[l], cache_d_kpe[l], *lw)
        xp = residual_update(xp, mp, *fw)
        xs = residual_update(xs, ms, *fw)
        p_states.append(sp)
        s_states.append(ss)
    p_a_k, p_a_v, p_b_k, p_b_v, p_b_logf, p_c_k, p_c_v, p_d_ckv, p_d_kpe = [jnp.stack(s) for s in zip(*p_states)]
    s_a_k, s_a_v, s_b_k, s_b_v, s_b_logf, s_c_k, s_c_v, s_d_ckv, s_d_kpe = [jnp.stack(s) for s in zip(*s_states)]
    y_prompt, y_sample = xp, xs
    return (y_prompt, y_sample, p_a_k, p_a_v, p_b_k, p_b_v, p_b_logf, p_c_k, p_c_v, p_d_ckv, p_d_kpe,
            s_a_k, s_a_v, s_b_k, s_b_v, s_b_logf, s_c_k, s_c_v, s_d_ckv, s_d_kpe)
```

```python
import functools

import numpy as np
import jax
import jax.numpy as jnp
from jax import lax
from jax.experimental import pallas as pl
from jax.experimental.pallas import tpu as pltpu

CHUNK = 64
A_HEADS = 8
L_HEADS = 4
BAND_CHUNKS = 8
REL_CLIP = 128
NOPE_DIM = 128
ROPE_DIM = 64
ROPE_THETA = 10000.0
RMS_EPS = 1e-6
NEG_INF = -1e30

LANE = 128
MXU_DEPTH = 256
VMEM_LIMIT = 56 * 1024 * 1024

F32 = jnp.float32
BF16 = jnp.bfloat16


def _cparams(*sem):
    return pltpu.CompilerParams(dimension_semantics=sem, vmem_limit_bytes=VMEM_LIMIT)


def _resident(shape):
    return pl.BlockSpec(shape, lambda *_: (0,) * len(shape), pipeline_mode=pl.Buffered(1))


def _rms(x, g):
    return x * lax.rsqrt(jnp.mean(x * x, axis=-1, keepdims=True) + RMS_EPS) * g


def _log_sigmoid(x):
    return jnp.minimum(x, 0.0) - jnp.log1p(jnp.exp(-jnp.abs(x)))


def _dot(a, b):
    return jnp.dot(a, b, preferred_element_type=F32)


def _dot_nt(a, b):
    return lax.dot_general(a, b, (((1,), (1,)), ((), ())), preferred_element_type=F32)


def _split2(x):
    hi = x.astype(BF16)
    lo = (x - hi.astype(F32)).astype(BF16)
    return hi, lo


def _split3(x):
    hi = x.astype(BF16)
    r = x - hi.astype(F32)
    mid = r.astype(BF16)
    lo = (r - mid.astype(F32)).astype(BF16)
    return hi, mid, lo


def _proj_kernel(x_ref, g_ref, w_ref, bf_ref, gq_ref, gkv_ref, wq_ref, wqs_ref, cos_ref, sin_ref,
                 qa_ref, ka_ref, va_ref, qb_ref, kb_ref, vb_ref, logf_ref, qc_ref, kc_ref, vc_ref,
                 qd_ref, ckv_ref, kpe_ref, kpe128_ref, *, gw, q_rank, kv_rank, rope_dim):
    h = _rms(x_ref[...], g_ref[...]).astype(BF16)

    def seg(a, n):
        return _dot(h, w_ref[:, a:a + n])

    groups = ((qa_ref, ka_ref, va_ref), (qb_ref, kb_ref, vb_ref), (qc_ref, kc_ref, vc_ref))
    for gi, (q_r, k_r, v_r) in enumerate(groups):
        base = 3 * gw * gi
        q_r[...] = seg(base, gw).astype(q_r.dtype)
        k_r[...] = seg(base + gw, gw)
        v_r[...] = seg(base + 2 * gw, gw)
    off = 9 * gw
    dq = seg(off, q_rank)
    off += q_rank
    dkv = seg(off, kv_rank)
    off += kv_rank
    dkr = seg(off, LANE)
    dkr_sw = seg(off + LANE, LANE)
    zf = seg(off + 2 * LANE, LANE)

    logf_ref[...] = _log_sigmoid(zf + bf_ref[...])
    ckv_ref[...] = _rms(dkv, gkv_ref[...])
    cos = cos_ref[...]
    sin = sin_ref[...]
    kpe = dkr * cos + dkr_sw * sin
    kpe128_ref[...] = kpe
    kpe_ref[...] = kpe[:, :rope_dim]

    nq = _rms(dq, gq_ref[...]).astype(BF16)
    qf = _dot(nq, wq_ref[...])
    qs = _dot(nq, wqs_ref[...])
    n_heads = qd_ref.shape[0]
    for hh in range(n_heads):
        qd_ref[hh, :, 0:LANE] = qf[:, LANE * hh:LANE * (hh + 1)].astype(BF16)
        r0 = LANE * (n_heads + hh)
        rot = qf[:, r0:r0 + LANE] * cos + qs[:, LANE * hh:LANE * (hh + 1)] * sin
        qd_ref[hh, :, LANE:2 * LANE] = rot.astype(BF16)


def _project(x, g, lw, cos_t, sin_t, tm):
    m, d = x.shape
    gw = d // 4
    npad = lw['w_in'].shape[1]
    q_rank, kv_rank = lw['g_q'].shape[1], lw['g_kv'].shape[1]
    n_pos_blocks = cos_t.shape[0] // tm
    rows = lambda i: (i, 0)
    pos = lambda i: (i % n_pos_blocks, 0)
    f32o = lambda n: jax.ShapeDtypeStruct((m, n), F32)
    bf16o = lambda n: jax.ShapeDtypeStruct((m, n), BF16)
    blk = lambda n: pl.BlockSpec((tm, n), rows)
    out_shape = [bf16o(gw), f32o(gw), f32o(gw), bf16o(gw), f32o(gw), f32o(gw), f32o(LANE),
                 bf16o(gw), f32o(gw), f32o(gw),
                 jax.ShapeDtypeStruct((L_HEADS, m, 2 * LANE), BF16),
                 f32o(kv_rank), f32o(ROPE_DIM), f32o(LANE)]
    out_specs = [blk(gw)] * 6 + [blk(LANE)] + [blk(gw)] * 3 + [
        pl.BlockSpec((L_HEADS, tm, 2 * LANE), lambda i: (0, i, 0)),
        blk(kv_rank), blk(ROPE_DIM), blk(LANE)]
    in_specs = [blk(d), _resident((1, d)), _resident((d, npad)), _resident((1, LANE)),
                _resident((1, q_rank)), _resident((1, kv_rank)),
                _resident(lw['w_q'].shape), _resident(lw['w_qs'].shape),
                pl.BlockSpec((tm, LANE), pos), pl.BlockSpec((tm, LANE), pos)]
    kern = functools.partial(_proj_kernel, gw=gw, q_rank=q_rank, kv_rank=kv_rank, rope_dim=ROPE_DIM)
    return pl.pallas_call(
        kern, grid=(m // tm,), in_specs=in_specs, out_specs=out_specs, out_shape=out_shape,
        compiler_params=_cparams("parallel"), name="in_proj",
    )(x, g, lw['w_in'], lw['b_f'], lw['g_q'], lw['g_kv'], lw['w_q'], lw['w_qs'], cos_t, sin_t)


def _mla_keys_kernel(ckv_ref, kpe_ref, wuk_ref, wuv_ref, kd_ref, vd_ref):
    c = ckv_ref[...].astype(BF16)
    kn = _dot(c, wuk_ref[...])
    vd_ref[...] = _dot(c, wuv_ref[...]).astype(BF16)
    kp = kpe_ref[...].astype(BF16)
    for hh in range(kd_ref.shape[0]):
        kd_ref[hh, :, 0:LANE] = kn[:, LANE * hh:LANE * (hh + 1)].astype(BF16)
        kd_ref[hh, :, LANE:2 * LANE] = kp


def _mla_keys(ckv, kpe128, w_uk, w_uv, tm):
    m, r = ckv.shape
    n = w_uk.shape[1]
    rows = lambda i: (i, 0)
    return pl.pallas_call(
        _mla_keys_kernel, grid=(m // tm,),
        in_specs=[pl.BlockSpec((tm, r), rows), pl.BlockSpec((tm, LANE), rows),
                  _resident(w_uk.shape), _resident(w_uv.shape)],
        out_specs=[pl.BlockSpec((L_HEADS, tm, 2 * LANE), lambda i: (0, i, 0)), pl.BlockSpec((tm, n), rows)],
        out_shape=[jax.ShapeDtypeStruct((L_HEADS, m, 2 * LANE), BF16), jax.ShapeDtypeStruct((m, n), BF16)],
        compiler_params=_cparams("parallel"), name="mla_keys",
    )(ckv, kpe128, w_uk, w_uv)


def _cumsum_kernel(x_ref, o_ref, carry_ref, *, seg, reverse):
    @pl.when(pl.program_id(1) == 0)
    def _():
        carry_ref[...] = jnp.zeros_like(carry_ref)

    x = x_ref[...]
    tc = x.shape[0]
    r = lax.broadcasted_iota(jnp.int32, (tc, tc), 0)
    c = lax.broadcasted_iota(jnp.int32, (tc, tc), 1)
    keep = (c > r) if reverse else (c <= r)
    if seg < tc:
        keep = keep & (r // seg == c // seg)
    tri = jnp.where(keep, 1.0, 0.0).astype(BF16)
    hi, mid, lo = _split3(x)
    y = _dot(tri, hi) + _dot(tri, mid) + _dot(tri, lo)
    o_ref[...] = y + carry_ref[...]
    if seg > tc:
        carry_ref[...] += jnp.sum(x, axis=0, keepdims=True)


def _cumsum(x, seg, reverse=False):
    m, n = x.shape
    tc = min(512, m) if seg >= 512 else min(256, m)
    nb = max(seg // tc, 1)
    nseg = m // (nb * tc)
    if reverse:
        idx = lambda s, j: (s * nb + nb - 1 - j, 0)
    else:
        idx = lambda s, j: (s * nb + j, 0)
    return pl.pallas_call(
        functools.partial(_cumsum_kernel, seg=seg, reverse=reverse), grid=(nseg, nb),
        in_specs=[pl.BlockSpec((tc, n), idx)], out_specs=pl.BlockSpec((tc, n), idx),
        out_shape=jax.ShapeDtypeStruct((m, n), F32),
        scratch_shapes=[pltpu.VMEM((1, n), F32)],
        compiler_params=_cparams("parallel", "arbitrary"), name="cumsum",
    )(x)


def _flash_kernel(*refs, scale, fox, bq):
    if fox:
        q_ref, k_ref, v_ref, fq_ref, fk_ref, o_ref, acc_ref, m_ref, l_ref = refs
    else:
        q_ref, k_ref, v_ref, o_ref, acc_ref, m_ref, l_ref = refs
    head = pl.program_id(1)
    qi = pl.program_id(2)
    q = q_ref[...]
    m_ref[...] = jnp.full_like(m_ref, NEG_INF)
    l_ref[...] = jnp.zeros_like(l_ref)
    acc_ref[...] = jnp.zeros_like(acc_ref)
    if fox:
        lane = lax.broadcasted_iota(jnp.int32, fq_ref.shape, 1)
        fq = jnp.sum(jnp.where(lane == head, fq_ref[...], 0.0), axis=-1, keepdims=True)

    def step(j, diagonal):
        start = pl.multiple_of(j * bq, bq)
        k = k_ref[pl.ds(start, bq), :].astype(BF16)
        v = v_ref[pl.ds(start, bq), :].astype(BF16)
        s = _dot_nt(q, k) * scale
        if fox:
            s = s + fq + fk_ref[pl.ds(j, 1), :]
        if diagonal:
            row = lax.broadcasted_iota(jnp.int32, s.shape, 0)
            col = lax.broadcasted_iota(jnp.int32, s.shape, 1)
            ok = (col <= row) if fox else (col // CHUNK <= row // CHUNK)
            s = jnp.where(ok, s, NEG_INF)
        m_prev = m_ref[...]
        m_new = jnp.maximum(m_prev, jnp.max(s, axis=-1, keepdims=True))
        alpha = jnp.exp(m_prev - m_new)
        p = jnp.exp(s - m_new)
        l_ref[...] = alpha * l_ref[...] + jnp.sum(p, axis=-1, keepdims=True)
        acc_ref[...] = alpha * acc_ref[...] + _dot(p.astype(BF16), v)
        m_ref[...] = m_new

    def body(j, carry):
        step(j, False)
        return carry

    lax.fori_loop(0, qi, body, 0)
    step(qi, True)
    o_ref[...] = (acc_ref[...] / l_ref[...]).astype(o_ref.dtype)


def _fox_prompt(q, k, v, f_rows, f_cols, batch, seq, bq):
    m, gw = q.shape
    nq = seq // bq
    qmap = lambda b, h, i: (b * nq + i, h)
    kvmap = lambda b, h, i: (b, h)
    return pl.pallas_call(
        functools.partial(_flash_kernel, scale=LANE ** -0.5, fox=True, bq=bq),
        grid=(batch, L_HEADS, nq),
        in_specs=[pl.BlockSpec((bq, LANE), qmap), pl.BlockSpec((seq, LANE), kvmap),
                  pl.BlockSpec((seq, LANE), kvmap),
                  pl.BlockSpec((bq, LANE), lambda b, h, i: (b * nq + i, 0)),
                  pl.BlockSpec((None, None, nq, bq), lambda b, h, i: (b, h, 0, 0))],
        out_specs=pl.BlockSpec((bq, LANE), qmap),
        out_shape=jax.ShapeDtypeStruct((m, gw), BF16),
        scratch_shapes=[pltpu.VMEM((bq, LANE), F32), pltpu.VMEM((bq, 1), F32), pltpu.VMEM((bq, 1), F32)],
        compiler_params=_cparams("parallel", "parallel", "arbitrary"), name="fox_prompt",
    )(q, k, v, f_rows, f_cols)


def _mla_prompt(qd, kd, vd, batch, seq, bq):
    n_heads, m, dk = qd.shape
    nq = seq // bq
    return pl.pallas_call(
        functools.partial(_flash_kernel, scale=(NOPE_DIM + ROPE_DIM) ** -0.5, fox=False, bq=bq),
        grid=(batch, n_heads, nq),
        in_specs=[pl.BlockSpec((None, bq, dk), lambda b, h, i: (h, b * nq + i, 0)),
                  pl.BlockSpec((None, seq, dk), lambda b, h, i: (h, b, 0)),
                  pl.BlockSpec((seq, LANE), lambda b, h, i: (b, h))],
        out_specs=pl.BlockSpec((bq, LANE), lambda b, h, i: (b * nq + i, h)),
        out_shape=jax.ShapeDtypeStruct((m, n_heads * LANE), BF16),
        scratch_shapes=[pltpu.VMEM((bq, LANE), F32), pltpu.VMEM((bq, 1), F32), pltpu.VMEM((bq, 1), F32)],
        compiler_params=_cparams("parallel", "parallel", "arbitrary"), name="mla_prompt",
    )(qd, kd, vd)


def _suffix_tri(n):
    r = lax.broadcasted_iota(jnp.int32, (n, n), 0)
    c = lax.broadcasted_iota(jnp.int32, (n, n), 1)
    return jnp.where(r >= c, 1.0, 0.0).astype(BF16)


def _sb_kernel(q_ref, k_ref, v_ref, o_ref, acc_ref, across_ref, *, scale, bq, sub):
    qi = pl.program_id(2)
    q = q_ref[...]
    acc_ref[...] = jnp.zeros_like(acc_ref)
    across_ref[...] = jnp.zeros_like(across_ref)
    tri = _suffix_tri(sub)

    def step(j, diagonal):
        start = pl.multiple_of(j * bq, bq)
        k = k_ref[pl.ds(start, bq), :].astype(BF16)
        z = _dot_nt(q, k) * scale
        log_beta = _log_sigmoid(z)
        log_fail = log_beta - z
        if diagonal:
            row = lax.broadcasted_iota(jnp.int32, z.shape, 0)
            col = lax.broadcasted_iota(jnp.int32, z.shape, 1)
            ok = col < row
            log_fail = jnp.where(ok, log_fail, 0.0)
        across = across_ref[...]
        acc = acc_ref[...]
        for c in reversed(range(bq // sub)):
            sl = slice(c * sub, (c + 1) * sub)
            lf = log_fail[:, sl]
            hi, lo = _split2(lf)
            incl = _dot(hi, tri) + _dot(lo, tri)
            w = jnp.exp(log_beta[:, sl] + (incl - lf + across))
            if diagonal:
                w = jnp.where(ok[:, sl], w, 0.0)
            vb = v_ref[pl.ds(pl.multiple_of(start + c * sub, sub), sub), :].astype(BF16)
            acc = acc + _dot(w.astype(BF16), vb)
            across = across + incl[:, 0:1]
        across_ref[...] = across
        acc_ref[...] = acc

    step(qi, True)

    def body(t, carry):
        step(qi - 1 - t, False)
        return carry

    lax.fori_loop(0, qi, body, 0)
    o_ref[...] = acc_ref[...].astype(o_ref.dtype)


def _sb_prompt(q, k, v, batch, seq, bq):
    m, gw = q.shape
    nq = seq // bq
    qmap = lambda b, h, i: (b * nq + i, h)
    kvmap = lambda b, h, i: (b, h)
    return pl.pallas_call(
        functools.partial(_sb_kernel, scale=LANE ** -0.5, bq=bq, sub=MXU_DEPTH),
        grid=(batch, L_HEADS, nq),
        in_specs=[pl.BlockSpec((bq, LANE), qmap), pl.BlockSpec((seq, LANE), kvmap),
                  pl.BlockSpec((seq, LANE), kvmap)],
        out_specs=pl.BlockSpec((bq, LANE), qmap),
        out_shape=jax.ShapeDtypeStruct((m, gw), BF16),
        scratch_shapes=[pltpu.VMEM((bq, LANE), F32), pltpu.VMEM((bq, 1), F32)],
        compiler_params=_cparams("parallel", "parallel", "arbitrary"), name="sb_prompt",
    )(q, k, v)


def _band_kernel(q_ref, kp_ref, kc_ref, vp_ref, vc_ref, t_ref, o_ref, *, scale, bq):
    qi = pl.program_id(2)
    q = q_ref[...]
    kp = kp_ref[...].astype(BF16)
    kc = kc_ref[...].astype(BF16)
    vp = vp_ref[...].astype(BF16)
    vc = vc_ref[...].astype(BF16)
    first_half = lax.broadcasted_iota(jnp.int32, q.shape, 1) < (LANE // 2)
    outs = []
    for hh in range(2):
        sel = first_half if hh == 0 else jnp.logical_not(first_half)
        qh = jnp.where(sel, q, jnp.zeros_like(q))
        sc = _dot_nt(qh, kc) * scale + t_ref[hh, :, bq:2 * bq]
        sp = _dot_nt(qh, kp) * scale + t_ref[hh, :, 0:bq]
        sp = jnp.where(qi > 0, sp, NEG_INF)
        mx = jnp.maximum(jnp.max(sc, axis=-1, keepdims=True), jnp.max(sp, axis=-1, keepdims=True))
        pc = jnp.exp(sc - mx)
        pp = jnp.exp(sp - mx)
        den = jnp.sum(pc, axis=-1, keepdims=True) + jnp.sum(pp, axis=-1, keepdims=True)
        outs.append((_dot(pc.astype(BF16), vc) + _dot(pp.astype(BF16), vp)) / den)
    o_ref[...] = jnp.where(first_half, outs[0], outs[1]).astype(o_ref.dtype)


def _band_table(rel_bias, bq):
    i = np.arange(bq)[:, None]
    j = np.arange(2 * bq)[None, :]
    rel = i + bq - j
    lo = (i // CHUNK) * CHUNK + bq - BAND_CHUNKS * CHUNK
    in_band = (j >= lo) & (j < (i // CHUNK) * CHUNK + bq + CHUNK)
    idx = np.clip(rel, -REL_CLIP, REL_CLIP) + REL_CLIP
    return jnp.where(jnp.asarray(in_band)[None], rel_bias[:, idx].astype(F32), NEG_INF)


def _band_prompt(q, k, v, table, batch, seq, bq):
    m, gw = q.shape
    nq = seq // bq
    cur = lambda p, b, i: (b * nq + i, p)
    prev = lambda p, b, i: (b * nq + jnp.maximum(i - 1, 0), p)
    blk = lambda f: pl.BlockSpec((bq, LANE), f)
    return pl.pallas_call(
        functools.partial(_band_kernel, scale=(LANE // 2) ** -0.5, bq=bq),
        grid=(A_HEADS // 2, batch, nq),
        in_specs=[blk(cur), blk(prev), blk(cur), blk(prev), blk(cur),
                  pl.BlockSpec((2, bq, 2 * bq), lambda p, b, i: (p, 0, 0))],
        out_specs=blk(cur),
        out_shape=jax.ShapeDtypeStruct((m, gw), BF16),
        compiler_params=_cparams("parallel", "parallel", "parallel"), name="band_prompt",
    )(q, k, k, v, v, table)


def _join_rows(cache, new, total):
    pad = total - cache.shape[0] - new.shape[0]
    parts = [cache.astype(BF16), new.astype(BF16)]
    if pad:
        parts.append(jnp.zeros((pad, cache.shape[1]), BF16))
    return jnp.concatenate(parts, axis=0)


def _band_sample_kernel(q_ref, kc_ref, vc_ref, kn_ref, vn_ref, t_ref, o_ref, *, scale, total):
    t_rows = q_ref.shape[0]
    first_half = lax.broadcasted_iota(jnp.int32, (t_rows, LANE), 1) < (LANE // 2)
    for p in range(A_HEADS // 2):
        cols = slice(p * LANE, (p + 1) * LANE)
        q = q_ref[:, cols]
        k = _join_rows(kc_ref[:, cols], kn_ref[:, cols], total)
        v = _join_rows(vc_ref[:, cols], vn_ref[:, cols], total)
        outs = []
        for hh in range(2):
            sel = first_half if hh == 0 else jnp.logical_not(first_half)
            qh = jnp.where(sel, q, jnp.zeros_like(q))
            s = _dot_nt(qh, k) * scale + t_ref[2 * p + hh]
            mx = jnp.max(s, axis=-1, keepdims=True)
            pr = jnp.exp(s - mx)
            outs.append(_dot(pr.astype(BF16), v) / jnp.sum(pr, axis=-1, keepdims=True))
        o_ref[:, cols] = jnp.where(first_half, outs[0], outs[1]).astype(o_ref.dtype)


def _softmax_sample_kernel(*refs, scale, fox, past, total, n_heads):
    if fox:
        q_ref, kc_ref, vc_ref, kn_ref, vn_ref, fq_ref, fk_ref, o_ref = refs
    else:
        q_ref, kc_ref, vc_ref, kn_ref, vn_ref, o_ref = refs
    t_rows = o_ref.shape[0]
    qpos = past + lax.broadcasted_iota(jnp.int32, (t_rows, total), 0)
    kpos = lax.broadcasted_iota(jnp.int32, (t_rows, total), 1)
    if fox:
        ok = kpos <= qpos
    else:
        ok = (kpos < past + t_rows) & (kpos // CHUNK <= qpos // CHUNK)
    for hh in range(n_heads):
        cols = slice(hh * LANE, (hh + 1) * LANE)
        if fox:
            q = q_ref[:, cols]
            k = _join_rows(kc_ref[:, cols], kn_ref[:, cols], total)
        else:
            q = q_ref[hh]
            k = _join_rows(kc_ref[hh], kn_ref[hh], total)
        v = _join_rows(vc_ref[:, cols], vn_ref[:, cols], total)
        s = _dot_nt(q, k) * scale
        if fox:
            s = s + fq_ref[:, hh:hh + 1] + fk_ref[hh:hh + 1, :]
        s = jnp.where(ok, s, NEG_INF)
        mx = jnp.max(s, axis=-1, keepdims=True)
        pr = jnp.exp(s - mx)
        o = _dot(pr.astype(BF16), v) / jnp.sum(pr, axis=-1, keepdims=True)
        o_ref[:, cols] = o.astype(o_ref.dtype)


def _sb_sample_kernel(q_ref, kc_ref, vc_ref, kn_ref, vn_ref, o_ref, *, scale, past, total, n_heads):
    t_rows = o_ref.shape[0]
    nb = total // LANE
    qpos = past + lax.broadcasted_iota(jnp.int32, (t_rows, total), 0)
    kpos = lax.broadcasted_iota(jnp.int32, (t_rows, total), 1)
    ok = kpos < qpos
    tri = _suffix_tri(LANE)
    for hh in range(n_heads):
        cols = slice(hh * LANE, (hh + 1) * LANE)
        k = _join_rows(kc_ref[:, cols], kn_ref[:, cols], total)
        v = _join_rows(vc_ref[:, cols], vn_ref[:, cols], total)
        z = _dot_nt(q_ref[:, cols], k) * scale
        log_beta = _log_sigmoid(z)
        log_fail = jnp.where(ok, log_beta - z, 0.0)
        stacked = jnp.concatenate([log_fail[:, c * LANE:(c + 1) * LANE] for c in range(nb)], axis=0)
        hi, lo = _split2(stacked)
        incl = _dot(hi, tri) + _dot(lo, tri)
        across = jnp.zeros((t_rows, 1), F32)
        w_blocks = [None] * nb
        for c in reversed(range(nb)):
            sl = slice(c * LANE, (c + 1) * LANE)
            inc = incl[c * t_rows:(c + 1) * t_rows]
            w = jnp.exp(log_beta[:, sl] + (inc - log_fail[:, sl] + across))
            w_blocks[c] = jnp.where(ok[:, sl], w, 0.0).astype(BF16)
            across = across + inc[:, 0:1]
        o_ref[:, cols] = _dot(jnp.concatenate(w_blocks, axis=-1), v).astype(o_ref.dtype)


def _sample_specs(n_streams, t_rows, past, gw):
    row = lambda b: (b, 0)
    return [pl.BlockSpec((t_rows, gw), row), pl.BlockSpec((past, gw), row), pl.BlockSpec((past, gw), row),
            pl.BlockSpec((t_rows, gw), row), pl.BlockSpec((t_rows, gw), row)]


def _pad_keys(n):
    return -(-n // LANE) * LANE


def _band_sample(q, kc, vc, kn, vn, table, n_streams):
    m, gw = q.shape
    t_rows, past = m // n_streams, kc.shape[0] // n_streams
    total = table.shape[-1]
    return pl.pallas_call(
        functools.partial(_band_sample_kernel, scale=(LANE // 2) ** -0.5, total=total),
        grid=(n_streams,),
        in_specs=_sample_specs(n_streams, t_rows, past, gw) + [_resident(table.shape)],
        out_specs=pl.BlockSpec((t_rows, gw), lambda b: (b, 0)),
        out_shape=jax.ShapeDtypeStruct((m, gw), BF16),
        compiler_params=_cparams("parallel"), name="band_sample",
    )(q, kc, vc, kn, vn, table)


def _fox_sample(q, kc, vc, kn, vn, f_rows, f_cols, n_streams):
    m, gw = q.shape
    t_rows, past = m // n_streams, kc.shape[0] // n_streams
    total = f_cols.shape[-1]
    return pl.pallas_call(
        functools.partial(_softmax_sample_kernel, scale=LANE ** -0.5, fox=True, past=past, total=total,
                          n_heads=L_HEADS),
        grid=(n_streams,),
        in_specs=_sample_specs(n_streams, t_rows, past, gw) + [
            pl.BlockSpec((t_rows, LANE), lambda b: (b, 0)),
            pl.BlockSpec((None, L_HEADS, total), lambda b: (b, 0, 0))],
        out_specs=pl.BlockSpec((t_rows, gw), lambda b: (b, 0)),
        out_shape=jax.ShapeDtypeStruct((m, gw), BF16),
        compiler_params=_cparams("parallel"), name="fox_sample",
    )(q, kc, vc, kn, vn, f_rows, f_cols)


def _sb_sample(q, kc, vc, kn, vn, n_streams):
    m, gw = q.shape
    t_rows, past = m // n_streams, kc.shape[0] // n_streams
    total = _pad_keys(past + t_rows)
    return pl.pallas_call(
        functools.partial(_sb_sample_kernel, scale=LANE ** -0.5, past=past, total=total, n_heads=L_HEADS),
        grid=(n_streams,),
        in_specs=_sample_specs(n_streams, t_rows, past, gw),
        out_specs=pl.BlockSpec((t_rows, gw), lambda b: (b, 0)),
        out_shape=jax.ShapeDtypeStruct((m, gw), BF16),
        compiler_params=_cparams("parallel"), name="sb_sample",
    )(q, kc, vc, kn, vn)


def _mla_sample(qd, kd_c, vd_c, kd_n, vd_n, n_streams):
    n_heads, m, dk = qd.shape
    gw = vd_n.shape[1]
    t_rows, past = m // n_streams, vd_c.shape[0] // n_streams
    total = _pad_keys(past + t_rows)
    lat = lambda rows: pl.BlockSpec((n_heads, rows, dk), lambda b: (0, b, 0))
    row = lambda b: (b, 0)
    return pl.pallas_call(
        functools.partial(_softmax_sample_kernel, scale=(NOPE_DIM + ROPE_DIM) ** -0.5, fox=False, past=past,
                          total=total, n_heads=n_heads),
        grid=(n_streams,),
        in_specs=[lat(t_rows), lat(past), pl.BlockSpec((past, gw), row), lat(t_rows),
                  pl.BlockSpec((t_rows, gw), row)],
        out_specs=pl.BlockSpec((t_rows, gw), row),
        out_shape=jax.ShapeDtypeStruct((m, gw), BF16),
        compiler_params=_cparams("parallel"), name="mla_sample",
    )(qd, kd_c, vd_c, kd_n, vd_n)


def _out_proj_kernel(oa_ref, ob_ref, oc_ref, od_ref, gmix_ref, wo_ref, x_ref, gpost_ref, gffn_ref,
                     x1_ref, h2_ref):
    gw = oa_ref.shape[1]
    y = None
    for gi, o_ref in enumerate((oa_ref, ob_ref, oc_ref, od_ref)):
        n = _rms(o_ref[...].astype(F32), gmix_ref[gi:gi + 1, :]).astype(BF16)
        part = _dot(n, wo_ref[gi * gw:(gi + 1) * gw, :])
        y = part if y is None else y + part
    x1 = x_ref[...] + _rms(y, gpost_ref[...])
    x1_ref[...] = x1
    h2_ref[...] = _rms(x1, gffn_ref[...]).astype(BF16)


def _out_proj(outs, g_mix, w_o, x, g_post, g_ffn, tm):
    m, d = x.shape
    gw = d // 4
    rows = lambda i: (i, 0)
    return pl.pallas_call(
        _out_proj_kernel, grid=(m // tm,),
        in_specs=[pl.BlockSpec((tm, gw), rows)] * 4 + [_resident(g_mix.shape), _resident(w_o.shape),
                                                       pl.BlockSpec((tm, d), rows),
                                                       _resident((1, d)), _resident((1, d))],
        out_specs=[pl.BlockSpec((tm, d), rows), pl.BlockSpec((tm, d), rows)],
        out_shape=[jax.ShapeDtypeStruct((m, d), F32), jax.ShapeDtypeStruct((m, d), BF16)],
        compiler_params=_cparams("parallel"), name="out_proj",
    )(*outs, g_mix, w_o, x, g_post, g_ffn)


def _ffn_kernel(h_ref, wg_ref, wu_ref, wd_ref, x_ref, g_ref, o_ref, acc_ref):
    f = pl.program_id(1)

    @pl.when(f == 0)
    def _():
        acc_ref[...] = jnp.zeros_like(acc_ref)

    h = h_ref[...]
    gate = _dot(h, wg_ref[...])
    up = _dot(h, wu_ref[...])
    act = (gate * jax.nn.sigmoid(gate) * up).astype(BF16)
    acc_ref[...] += _dot(act, wd_ref[...])

    @pl.when(f == pl.num_programs(1) - 1)
    def _():
        o_ref[...] = x_ref[...] + _rms(acc_ref[...], g_ref[...])


def _ffn(h2, w_gate, w_up, w_down, x1, g, tm, tf):
    m, d = x1.shape
    ff = w_gate.shape[1]
    rows = lambda i, f: (i, 0)
    return pl.pallas_call(
        _ffn_kernel, grid=(m // tm, ff // tf),
        in_specs=[pl.BlockSpec((tm, d), rows), pl.BlockSpec((d, tf), lambda i, f: (0, f)),
                  pl.BlockSpec((d, tf), lambda i, f: (0, f)), pl.BlockSpec((tf, d), lambda i, f: (f, 0)),
                  pl.BlockSpec((tm, d), rows), pl.BlockSpec((1, d), lambda i, f: (0, 0))],
        out_specs=pl.BlockSpec((tm, d), rows),
        out_shape=jax.ShapeDtypeStruct((m, d), F32),
        scratch_shapes=[pltpu.VMEM((tm, d), F32)],
        compiler_params=_cparams("parallel", "arbitrary"), name="ffn",
    )(h2, w_gate, w_up, w_down, x1, g)


def _pad_cols(w, n):
    return jnp.pad(w, ((0, 0), (0, n - w.shape[1])))


def _swap_halves(w):
    half = w.shape[1] // 2
    return jnp.concatenate([w[:, half:], w[:, :half]], axis=1)


def _layer_weights(l, d, w_in, b_f, g_q_lat, g_kv_lat, w_uq, w_uk, w_uv):
    gw = d // 4
    q_rank, kv_rank = g_q_lat.shape[1], g_kv_lat.shape[1]
    sizes = (3 * gw, 3 * gw, L_HEADS, 3 * gw, q_rank, kv_rank, ROPE_DIM)
    za, zb, zf, zc, dq, dkv, dkr = jnp.split(w_in[l], np.cumsum(sizes)[:-1].tolist(), axis=1)
    w_all = jnp.concatenate([za, zb, zc, dq, dkv, _pad_cols(dkr, LANE), _pad_cols(_swap_halves(dkr), LANE),
                             _pad_cols(zf, LANE)], axis=1).astype(BF16)
    hd = NOPE_DIM + ROPE_DIM
    uq = w_uq[l]
    nope = [uq[:, h * hd:h * hd + NOPE_DIM] for h in range(L_HEADS)]
    rope = [uq[:, h * hd + NOPE_DIM:(h + 1) * hd] for h in range(L_HEADS)]
    w_q = jnp.concatenate(nope + [_pad_cols(r, LANE) for r in rope], axis=1).astype(BF16)
    w_qs = jnp.concatenate([_pad_cols(_swap_halves(r), LANE) for r in rope], axis=1).astype(BF16)
    return dict(w_in=w_all, b_f=_pad_cols(b_f[l][None, :].astype(F32), LANE), g_q=g_q_lat[l][None, :],
                g_kv=g_kv_lat[l][None, :], w_q=w_q, w_qs=w_qs,
                w_uk=w_uk[l].astype(BF16), w_uv=w_uv[l].astype(BF16))


def _rope_tables(pos):
    half = ROPE_DIM // 2
    inv_freq = ROPE_THETA ** (-jnp.arange(half, dtype=F32) / half)
    ang = pos.astype(F32)[:, None] * inv_freq[None, :]
    cos, sin = jnp.cos(ang), jnp.sin(ang)
    zero = jnp.zeros((pos.shape[0], LANE - ROPE_DIM), F32)
    return jnp.concatenate([cos, cos, zero], axis=1), jnp.concatenate([-sin, sin, zero], axis=1)


def kernel(x_prompt, x_sample, cache_a_k, cache_a_v, cache_b_k, cache_b_v, cache_b_logf, cache_c_k, cache_c_v,
           cache_d_ckv, cache_d_kpe, w_in, b_f, rel_bias, g_q_lat, g_kv_lat, w_uq, w_uk, w_uv, g_mix, w_o,
           g_pre_attn, g_post_attn, g_pre_ffn, g_post_ffn, w_gate, w_up, w_down):
    batch, seq, d = x_prompt.shape
    n_streams, t_rows, _ = x_sample.shape
    depth = w_in.shape[0]
    past = cache_b_k.shape[2]
    a_win = cache_a_k.shape[2]
    gw = d // 4
    mp, ms = batch * seq, n_streams * t_rows
    bq = 512
    tm_p, tm_s = 256, ms

    cos_p, sin_p = _rope_tables(jnp.arange(seq))
    cos_s, sin_s = _rope_tables(jnp.tile(past + jnp.arange(t_rows), n_streams))

    xp = x_prompt.reshape(mp, d)
    xs = x_sample.reshape(ms, d)
    p_states, s_states = [], []
    for l in range(depth):
        lw = _layer_weights(l, d, w_in, b_f, g_q_lat, g_kv_lat, w_uq, w_uk, w_uv)
        g_attn = g_pre_attn[l][None, :]
        wo = w_o[l].astype(BF16)
        wg, wu, wd = w_gate[l].astype(BF16), w_up[l].astype(BF16), w_down[l].astype(BF16)
        g_post, g_ffn, g_out = g_post_attn[l][None, :], g_pre_ffn[l][None, :], g_post_ffn[l][None, :]

        (qa, ka, va, qb, kb, vb, logf, qc, kc, vc, qd, ckv, kpe, kpe128) = _project(
            xp, g_attn, lw, cos_p, sin_p, tm_p)
        f_run = _cumsum(logf, seq)
        f_cols = (-f_run[:, :L_HEADS]).reshape(batch, seq, L_HEADS).transpose(0, 2, 1)
        f_cols = f_cols.reshape(batch, L_HEADS, seq // bq, bq)
        kd, vd = _mla_keys(ckv, kpe128, lw['w_uk'], lw['w_uv'], 512)
        oa = _band_prompt(qa, ka, va, _band_table(rel_bias[l], bq), batch, seq, bq)
        ob = _fox_prompt(qb, kb, vb, f_run, f_cols, batch, seq, bq)
        oc = _sb_prompt(qc, kc, vc, batch, seq, bq)
        od = _mla_prompt(qd, kd, vd, batch, seq, bq)
        x1, h2 = _out_proj((oa, ob, oc, od), g_mix[l], wo, xp, g_post, g_ffn, tm_p)
        xp = _ffn(h2, wg, wu, wd, x1, g_out, 512, 512)
        keep = min(a_win, seq)
        p_states.append((
            ka.reshape(batch, seq, A_HEADS, gw // A_HEADS)[:, seq - keep:],
            va.reshape(batch, seq, A_HEADS, gw // A_HEADS)[:, seq - keep:],
            kb.reshape(batch, seq, L_HEADS, LANE), vb.reshape(batch, seq, L_HEADS, LANE),
            logf[:, :L_HEADS].reshape(batch, seq, L_HEADS),
            kc.reshape(batch, seq, L_HEADS, LANE), vc.reshape(batch, seq, L_HEADS, LANE),
            ckv.reshape(batch, seq, -1), kpe.reshape(batch, seq, ROPE_DIM)))

        (qa, ka, va, qb, kb, vb, logf, qc, kc, vc, qd, ckv, kpe, kpe128) = _project(
            xs, g_attn, lw, cos_s, sin_s, tm_s)
        tot_a = _pad_keys(a_win + t_rows)
        rel = (a_win + np.arange(t_rows))[:, None] - np.arange(tot_a)[None, :]
        tab = rel_bias[l][:, np.clip(rel, -REL_CLIP, REL_CLIP) + REL_CLIP].astype(F32)
        tab = jnp.where(jnp.asarray(np.arange(tot_a) < a_win + t_rows)[None, None, :], tab, NEG_INF)
        oa = _band_sample(qa, cache_a_k[l].reshape(n_streams * a_win, gw),
                          cache_a_v[l].reshape(n_streams * a_win, gw), ka, va, tab, n_streams)
        f_new = _cumsum(logf, t_rows)
        cl = _pad_cols(cache_b_logf[l].reshape(n_streams * past, L_HEADS).astype(F32), LANE)
        suffix = _cumsum(cl, past, reverse=True)
        tot = _pad_keys(past + t_rows)
        f_cols = jnp.concatenate([suffix[:, :L_HEADS].reshape(n_streams, past, L_HEADS),
                                  -f_new[:, :L_HEADS].reshape(n_streams, t_rows, L_HEADS),
                                  jnp.zeros((n_streams, tot - past - t_rows, L_HEADS), F32)], axis=1)
        f_cols = f_cols.transpose(0, 2, 1)
        ob = _fox_sample(qb, cache_b_k[l].reshape(n_streams * past, gw),
                         cache_b_v[l].reshape(n_streams * past, gw), kb, vb, f_new, f_cols, n_streams)
        oc = _sb_sample(qc, cache_c_k[l].reshape(n_streams * past, gw),
                        cache_c_v[l].reshape(n_streams * past, gw), kc, vc, n_streams)
        kd_n, vd_n = _mla_keys(ckv, kpe128, lw['w_uk'], lw['w_uv'], ms)
        kd_c, vd_c = _mla_keys(cache_d_ckv[l].reshape(n_streams * past, -1),
                               _pad_cols(cache_d_kpe[l].reshape(n_streams * past, ROPE_DIM), LANE),
                               lw['w_uk'], lw['w_uv'], 512)
        od = _mla_sample(qd, kd_c, vd_c, kd_n, vd_n, n_streams)
        x1, h2 = _out_proj((oa, ob, oc, od), g_mix[l], wo, xs, g_post, g_ffn, tm_s)
        xs = _ffn(h2, wg, wu, wd, x1, g_out, ms, 512)
        s_states.append((
            ka.reshape(n_streams, t_rows, A_HEADS, gw // A_HEADS),
            va.reshape(n_streams, t_rows, A_HEADS, gw // A_HEADS),
            kb.reshape(n_streams, t_rows, L_HEADS, LANE), vb.reshape(n_streams, t_rows, L_HEADS, LANE),
            logf[:, :L_HEADS].reshape(n_streams, t_rows, L_HEADS),
            kc.reshape(n_streams, t_rows, L_HEADS, LANE), vc.reshape(n_streams, t_rows, L_HEADS, LANE),
            ckv.reshape(n_streams, t_rows, -1), kpe.reshape(n_streams, t_rows, ROPE_DIM)))

    p_out = [jnp.stack(s) for s in zip(*p_states)]
    s_out = [jnp.stack(s) for s in zip(*s_states)]
    return (xp.reshape(batch, seq, d), xs.reshape(n_streams, t_rows, d), *p_out, *s_out)
```

```python
import functools

import numpy as np
import jax
import jax.numpy as jnp
from jax import lax
from jax.experimental import pallas as pl
from jax.experimental.pallas import tpu as pltpu

CHUNK = 64
A_HEADS = 8
L_HEADS = 4
BAND_CHUNKS = 8
REL_CLIP = 128
NOPE_DIM = 128
ROPE_DIM = 64
ROPE_THETA = 10000.0
RMS_EPS = 1e-6
NEG_INF = -1e30

LANE = 128
MXU_DEPTH = 256
VMEM_LIMIT = 56 * 1024 * 1024

F32 = jnp.float32
BF16 = jnp.bfloat16


def _cparams(*sem):
    return pltpu.CompilerParams(dimension_semantics=sem, vmem_limit_bytes=VMEM_LIMIT)


def _resident(shape):
    return pl.BlockSpec(shape, lambda *_: (0,) * len(shape), pipeline_mode=pl.Buffered(1))


def _rms(x, g):
    return x * lax.rsqrt(jnp.mean(x * x, axis=-1, keepdims=True) + RMS_EPS) * g


def _log_sigmoid(x):
    return jnp.minimum(x, 0.0) - jnp.log1p(jnp.exp(-jnp.abs(x)))


def _dot(a, b):
    return jnp.dot(a, b, preferred_element_type=F32)


def _dot_nt(a, b):
    return lax.dot_general(a, b, (((1,), (1,)), ((), ())), preferred_element_type=F32)


def _split2(x):
    hi = x.astype(BF16)
    lo = (x - hi.astype(F32)).astype(BF16)
    return hi, lo


def _split3(x):
    hi = x.astype(BF16)
    r = x - hi.astype(F32)
    mid = r.astype(BF16)
    lo = (r - mid.astype(F32)).astype(BF16)
    return hi, mid, lo


def _proj_kernel(x_ref, g_ref, w_ref, bf_ref, gq_ref, gkv_ref, wq_ref, wqs_ref, cos_ref, sin_ref,
                 qa_ref, ka_ref, va_ref, ka16_ref, va16_ref, qb_ref, kb_ref, vb_ref, kb16_ref, vb16_ref,
                 qc_ref, kc_ref, vc_ref, kc16_ref, vc16_ref, logf_ref, qd_ref, ckv_ref, kpe_ref, kpe128_ref,
                 *, gw, q_rank, kv_rank, rope_dim):
    h = _rms(x_ref[...], g_ref[...]).astype(BF16)

    def seg(a, n):
        return _dot(h, w_ref[:, a:a + n])

    groups = ((qa_ref, ka_ref, va_ref, ka16_ref, va16_ref), (qb_ref, kb_ref, vb_ref, kb16_ref, vb16_ref),
              (qc_ref, kc_ref, vc_ref, kc16_ref, vc16_ref))
    for gi, (q_r, k_r, v_r, k16_r, v16_r) in enumerate(groups):
        base = 3 * gw * gi
        q_r[...] = seg(base, gw).astype(BF16)
        k = seg(base + gw, gw)
        k_r[...] = k
        k16_r[...] = k.astype(BF16)
        v = seg(base + 2 * gw, gw)
        v_r[...] = v
        v16_r[...] = v.astype(BF16)
    off = 9 * gw
    dq = seg(off, q_rank)
    off += q_rank
    dkv = seg(off, kv_rank)
    off += kv_rank
    dkr = seg(off, LANE)
    dkr_sw = seg(off + LANE, LANE)
    zf = seg(off + 2 * LANE, LANE)

    logf_ref[...] = _log_sigmoid(zf + bf_ref[...])
    ckv_ref[...] = _rms(dkv, gkv_ref[...])
    cos = cos_ref[...]
    sin = sin_ref[...]
    kpe = dkr * cos + dkr_sw * sin
    kpe128_ref[...] = kpe
    kpe_ref[...] = kpe[:, :rope_dim]

    nq = _rms(dq, gq_ref[...]).astype(BF16)
    qf = _dot(nq, wq_ref[...])
    qs = _dot(nq, wqs_ref[...])
    n_heads = qd_ref.shape[0]
    for hh in range(n_heads):
        qd_ref[hh, :, 0:LANE] = qf[:, LANE * hh:LANE * (hh + 1)].astype(BF16)
        r0 = LANE * (n_heads + hh)
        rot = qf[:, r0:r0 + LANE] * cos + qs[:, LANE * hh:LANE * (hh + 1)] * sin
        qd_ref[hh, :, LANE:2 * LANE] = rot.astype(BF16)


_PROJ_OUTPUTS = ('qa', 'ka', 'va', 'ka16', 'va16', 'qb', 'kb', 'vb', 'kb16', 'vb16',
                 'qc', 'kc', 'vc', 'kc16', 'vc16', 'logf', 'qd', 'ckv', 'kpe', 'kpe128')


def _project(x, g, lw, cos_t, sin_t, tm):
    m, d = x.shape
    gw = d // 4
    npad = lw['w_in'].shape[1]
    q_rank, kv_rank = lw['g_q'].shape[1], lw['g_kv'].shape[1]
    n_pos_blocks = cos_t.shape[0] // tm
    rows = lambda i: (i, 0)
    pos = lambda i: (i % n_pos_blocks, 0)
    f32o = lambda n: jax.ShapeDtypeStruct((m, n), F32)
    bf16o = lambda n: jax.ShapeDtypeStruct((m, n), BF16)
    blk = lambda n: pl.BlockSpec((tm, n), rows)
    group = [bf16o(gw), f32o(gw), f32o(gw), bf16o(gw), bf16o(gw)]
    out_shape = group * 3 + [f32o(LANE), jax.ShapeDtypeStruct((L_HEADS, m, 2 * LANE), BF16),
                             f32o(kv_rank), f32o(ROPE_DIM), f32o(LANE)]
    out_specs = [blk(gw)] * 15 + [blk(LANE), pl.BlockSpec((L_HEADS, tm, 2 * LANE), lambda i: (0, i, 0)),
                                  blk(kv_rank), blk(ROPE_DIM), blk(LANE)]
    in_specs = [blk(d), _resident((1, d)), _resident((d, npad)), _resident((1, LANE)),
                _resident((1, q_rank)), _resident((1, kv_rank)),
                _resident(lw['w_q'].shape), _resident(lw['w_qs'].shape),
                pl.BlockSpec((tm, LANE), pos), pl.BlockSpec((tm, LANE), pos)]
    kern = functools.partial(_proj_kernel, gw=gw, q_rank=q_rank, kv_rank=kv_rank, rope_dim=ROPE_DIM)
    outs = pl.pallas_call(
        kern, grid=(m // tm,), in_specs=in_specs, out_specs=out_specs, out_shape=out_shape,
        compiler_params=_cparams("parallel"), name="in_proj",
    )(x, g, lw['w_in'], lw['b_f'], lw['g_q'], lw['g_kv'], lw['w_q'], lw['w_qs'], cos_t, sin_t)
    return dict(zip(_PROJ_OUTPUTS, outs))


def _mla_keys_kernel(ckv_ref, kpe_ref, wuk_ref, wuv_ref, kd_ref, vd_ref):
    c = ckv_ref[...].astype(BF16)
    kn = _dot(c, wuk_ref[...])
    vd_ref[...] = _dot(c, wuv_ref[...]).astype(BF16)
    kp = kpe_ref[...].astype(BF16)
    for hh in range(kd_ref.shape[0]):
        kd_ref[hh, :, 0:LANE] = kn[:, LANE * hh:LANE * (hh + 1)].astype(BF16)
        kd_ref[hh, :, LANE:2 * LANE] = kp


def _mla_keys(ckv, kpe128, w_uk, w_uv, tm):
    m, r = ckv.shape
    n = w_uk.shape[1]
    rows = lambda i: (i, 0)
    return pl.pallas_call(
        _mla_keys_kernel, grid=(m // tm,),
        in_specs=[pl.BlockSpec((tm, r), rows), pl.BlockSpec((tm, LANE), rows),
                  _resident(w_uk.shape), _resident(w_uv.shape)],
        out_specs=[pl.BlockSpec((L_HEADS, tm, 2 * LANE), lambda i: (0, i, 0)), pl.BlockSpec((tm, n), rows)],
        out_shape=[jax.ShapeDtypeStruct((L_HEADS, m, 2 * LANE), BF16), jax.ShapeDtypeStruct((m, n), BF16)],
        compiler_params=_cparams("parallel"), name="mla_keys",
    )(ckv, kpe128, w_uk, w_uv)


def _cumsum_kernel(x_ref, o_ref, carry_ref, *, seg, reverse):
    @pl.when(pl.program_id(1) == 0)
    def _():
        carry_ref[...] = jnp.zeros_like(carry_ref)

    x = x_ref[...]
    tc = x.shape[0]
    r = lax.broadcasted_iota(jnp.int32, (tc, tc), 0)
    c = lax.broadcasted_iota(jnp.int32, (tc, tc), 1)
    keep = (c > r) if reverse else (c <= r)
    if seg < tc:
        keep = keep & (r // seg == c // seg)
    tri = jnp.where(keep, 1.0, 0.0).astype(BF16)
    hi, mid, lo = _split3(x)
    y = _dot(tri, hi) + _dot(tri, mid) + _dot(tri, lo)
    o_ref[...] = y + carry_ref[...]
    if seg > tc:
        carry_ref[...] += jnp.sum(x, axis=0, keepdims=True)


def _cumsum(x, seg, reverse=False):
    m, n = x.shape
    tc = min(512, m) if seg >= 512 else min(256, m)
    nb = max(seg // tc, 1)
    nseg = m // (nb * tc)
    if reverse:
        idx = lambda s, j: (s * nb + nb - 1 - j, 0)
    else:
        idx = lambda s, j: (s * nb + j, 0)
    return pl.pallas_call(
        functools.partial(_cumsum_kernel, seg=seg, reverse=reverse), grid=(nseg, nb),
        in_specs=[pl.BlockSpec((tc, n), idx)], out_specs=pl.BlockSpec((tc, n), idx),
        out_shape=jax.ShapeDtypeStruct((m, n), F32),
        scratch_shapes=[pltpu.VMEM((1, n), F32)],
        compiler_params=_cparams("parallel", "arbitrary"), name="cumsum",
    )(x)


def _lane_tile(x, n):
    return x if n == LANE else jnp.concatenate([x] * (n // LANE), axis=1)


FLASH_AHEAD = 3


def _flash_kernel(*refs, scale, fox, bq, bk, rb):
    if fox:
        q_ref, k_ref, v_ref, fq_ref, fk_ref, o_ref, acc_ref, m_ref, l_ref = refs
    else:
        q_ref, k_ref, v_ref, o_ref, acc_ref, m_ref, l_ref = refs
    head = pl.program_id(1)
    qi = pl.program_id(2)
    m_ref[...] = jnp.full_like(m_ref, NEG_INF)
    l_ref[...] = jnp.zeros_like(l_ref)
    acc_ref[...] = jnp.zeros_like(acc_ref)
    if fox:
        lane = lax.broadcasted_iota(jnp.int32, fq_ref.shape, 1)
        fq = jnp.sum(jnp.where(lane == head, fq_ref[...], 0.0), axis=-1, keepdims=True)

    def logits(r, j):
        rows = slice(r * rb, (r + 1) * rb)
        s = _dot_nt(q_ref[rows, :], k_ref[pl.ds(pl.multiple_of(j * bk, bk), bk), :]) * scale
        if fox:
            s = s + fq[rows] + fk_ref[pl.ds(j, 1), :]
        return s

    def update(r, s, j):
        rows = slice(r * rb, (r + 1) * rb)
        m_prev = m_ref[rows, :]
        m_new = jnp.maximum(m_prev, jnp.max(s, axis=-1, keepdims=True))
        alpha = jnp.exp(m_prev - m_new)
        p = jnp.exp(s - _lane_tile(m_new, s.shape[1]))
        l_ref[rows, :] = alpha * l_ref[rows, :] + jnp.sum(p, axis=-1, keepdims=True)
        v = v_ref[pl.ds(pl.multiple_of(j * bk, bk), bk), :]
        acc_ref[rows, :] = alpha * acc_ref[rows, :] + _dot(p.astype(BF16), v)
        m_ref[rows, :] = m_new

    def run(items):
        pending = [logits(r, j) for r, j, _ in items[:FLASH_AHEAD]]
        for i, (r, j, diag) in enumerate(items):
            if i + FLASH_AHEAD < len(items):
                pending.append(logits(*items[i + FLASH_AHEAD][:2]))
            s = pending.pop(0)
            if diag is not None:
                row = r * rb + lax.broadcasted_iota(jnp.int32, s.shape, 0)
                col = diag + lax.broadcasted_iota(jnp.int32, s.shape, 1)
                ok = (col <= row) if fox else (col // CHUNK <= row // CHUNK)
                s = jnp.where(ok, s, NEG_INF)
            update(r, s, j)

    n_groups = bq // rb
    kb_per_q = bq // bk

    def full_block(j, carry):
        run([(r, j, None) for r in range(n_groups)])
        return carry

    lax.fori_loop(0, qi * kb_per_q, full_block, 0)

    items = []
    for kb in range(kb_per_q):
        first = kb * bk // rb
        items += [(r, qi * kb_per_q + kb, kb * bk) for r in range(first, first + bk // rb)]
        items += [(r, qi * kb_per_q + kb, None) for r in range(first + bk // rb, n_groups)]
    run(items)
    o_ref[...] = (acc_ref[...] / l_ref[...]).astype(o_ref.dtype)


def _flash_scratch(bq):
    return [pltpu.VMEM((bq, LANE), F32), pltpu.VMEM((bq, LANE), F32), pltpu.VMEM((bq, LANE), F32)]


def _fox_prompt(q, k, v, f_rows, f_cols, batch, seq, bq, bk, rb):
    m, gw = q.shape
    nq = seq // bq
    qmap = lambda b, h, i: (b * nq + i, h)
    kvmap = lambda b, h, i: (b, h)
    return pl.pallas_call(
        functools.partial(_flash_kernel, scale=LANE ** -0.5, fox=True, bq=bq, bk=bk, rb=rb),
        grid=(batch, L_HEADS, nq),
        in_specs=[pl.BlockSpec((bq, LANE), qmap), pl.BlockSpec((seq, LANE), kvmap),
                  pl.BlockSpec((seq, LANE), kvmap),
                  pl.BlockSpec((bq, LANE), lambda b, h, i: (b * nq + i, 0)),
                  pl.BlockSpec((None, None, seq // bk, bk), lambda b, h, i: (b, h, 0, 0))],
        out_specs=pl.BlockSpec((bq, LANE), qmap),
        out_shape=jax.ShapeDtypeStruct((m, gw), BF16),
        scratch_shapes=_flash_scratch(bq),
        compiler_params=_cparams("parallel", "parallel", "arbitrary"), name="fox_prompt",
    )(q, k, v, f_rows, f_cols)


def _mla_prompt(qd, kd, vd, batch, seq, bq, bk, rb):
    n_heads, m, dk = qd.shape
    nq = seq // bq
    return pl.pallas_call(
        functools.partial(_flash_kernel, scale=(NOPE_DIM + ROPE_DIM) ** -0.5, fox=False, bq=bq, bk=bk, rb=rb),
        grid=(batch, n_heads, nq),
        in_specs=[pl.BlockSpec((None, bq, dk), lambda b, h, i: (h, b * nq + i, 0)),
                  pl.BlockSpec((None, seq, dk), lambda b, h, i: (h, b, 0)),
                  pl.BlockSpec((seq, LANE), lambda b, h, i: (b, h))],
        out_specs=pl.BlockSpec((bq, LANE), lambda b, h, i: (b * nq + i, h)),
        out_shape=jax.ShapeDtypeStruct((m, n_heads * LANE), BF16),
        scratch_shapes=_flash_scratch(bq),
        compiler_params=_cparams("parallel", "parallel", "arbitrary"), name="mla_prompt",
    )(qd, kd, vd)


SB_UNDERFLOW = 105.0


def _suffix_tri(n):
    r = lax.broadcasted_iota(jnp.int32, (n, n), 0)
    c = lax.broadcasted_iota(jnp.int32, (n, n), 1)
    return jnp.where(r >= c, 1.0, 0.0).astype(BF16)


def _sb_kernel(q_ref, k_ref, v_ref, o_ref, acc_ref, across_ref, *, scale, bq, rb, sub):
    qi = pl.program_id(2)
    acc_ref[...] = jnp.zeros_like(acc_ref)
    across_ref[...] = jnp.zeros_like(across_ref)
    tri = _suffix_tri(sub)

    n_groups = bq // rb

    def visit(start, diagonal):
        n_keys = [-(-((r + 1) * rb) // sub) * sub if diagonal else bq for r in range(n_groups)]

        def stage_logits(r):
            rows = slice(r * rb, (r + 1) * rb)
            return _dot_nt(q_ref[rows, :], k_ref[pl.ds(start, n_keys[r]), :])

        def stage_sums(r, z):
            z = z * scale
            log_fail = -jnp.maximum(z, 0.0) - jnp.log(1.0 + jnp.exp(-jnp.abs(z)))
            log_beta = log_fail + z
            ok = None
            if diagonal:
                row = r * rb + lax.broadcasted_iota(jnp.int32, z.shape, 0)
                col = lax.broadcasted_iota(jnp.int32, z.shape, 1)
                ok = col < row
                log_fail = jnp.where(ok, log_fail, 0.0)
            incl = {}
            for c in reversed(range(n_keys[r] // sub)):
                hi, lo = _split2(log_fail[:, c * sub:(c + 1) * sub])
                incl[c] = _dot(hi, tri) + _dot(lo, tri)
            return log_fail, log_beta, ok, incl

        def stage_values(r, state):
            log_fail, log_beta, ok, incl = state
            rows = slice(r * rb, (r + 1) * rb)
            across = across_ref[rows, :]
            acc = acc_ref[rows, :]
            for c in reversed(range(n_keys[r] // sub)):
                sl = slice(c * sub, (c + 1) * sub)
                w = jnp.exp(log_beta[:, sl] + (incl[c] - log_fail[:, sl] + _lane_tile(across, sub)))
                if diagonal:
                    w = jnp.where(ok[:, sl], w, 0.0)
                vb = v_ref[pl.ds(pl.multiple_of(start + c * sub, sub), sub), :]
                acc = acc + _dot(w.astype(BF16), vb)
                across = across + jnp.broadcast_to(incl[c][:, 0:1], across.shape)
            across_ref[rows, :] = across
            acc_ref[rows, :] = acc

        zs = [stage_logits(r) for r in range(n_groups)]
        states = [stage_sums(r, zs[r]) for r in range(min(2, n_groups))]
        for r in range(n_groups):
            stage_values(r, states[r])
            if r + 2 < n_groups:
                states.append(stage_sums(r + 2, zs[r + 2]))

    visit(pl.multiple_of(qi * bq, bq), True)

    def more(carry):
        t, worst = carry
        return jnp.logical_and(t < qi, worst > -SB_UNDERFLOW)

    def earlier_block(carry):
        t, _ = carry
        visit(pl.multiple_of((qi - 1 - t) * bq, bq), False)
        return t + 1, jnp.max(across_ref[...])

    lax.while_loop(more, earlier_block, (jnp.int32(0), jnp.max(across_ref[...])))
    o_ref[...] = acc_ref[...].astype(o_ref.dtype)


def _sb_prompt(q, k, v, batch, seq, bq, rb):
    m, gw = q.shape
    nq = seq // bq
    qmap = lambda b, h, i: (b * nq + i, h)
    kvmap = lambda b, h, i: (b, h)
    return pl.pallas_call(
        functools.partial(_sb_kernel, scale=LANE ** -0.5, bq=bq, rb=rb, sub=MXU_DEPTH),
        grid=(batch, L_HEADS, nq),
        in_specs=[pl.BlockSpec((bq, LANE), qmap), pl.BlockSpec((seq, LANE), kvmap),
                  pl.BlockSpec((seq, LANE), kvmap)],
        out_specs=pl.BlockSpec((bq, LANE), qmap),
        out_shape=jax.ShapeDtypeStruct((m, gw), BF16),
        scratch_shapes=[pltpu.VMEM((bq, LANE), F32), pltpu.VMEM((bq, LANE), F32)],
        compiler_params=_cparams("parallel", "parallel", "arbitrary"), name="sb_prompt",
    )(q, k, v)


BAND_AHEAD = 2


def _band_kernel(q_ref, kp_ref, kc_ref, vp_ref, vc_ref, t_ref, o_ref, *, scale, bq, rb):
    has_prev = pl.program_id(2) > 0
    first_half = lax.broadcasted_iota(jnp.int32, (rb, LANE), 1) < (LANE // 2)

    def band_cols(r):
        lo = bq - BAND_CHUNKS * CHUNK + r * rb
        n_cur = (r + 1) * rb
        return lo, n_cur

    def stage_logits(r, hh):
        rows = slice(r * rb, (r + 1) * rb)
        lo, n_cur = band_cols(r)
        q = q_ref[rows, :]
        sel = first_half if hh == 0 else jnp.logical_not(first_half)
        qh = jnp.where(sel, q, jnp.zeros_like(q))
        return _dot_nt(qh, kc_ref[0:n_cur, :]), _dot_nt(qh, kp_ref[lo:bq, :])

    def stage_softmax(r, hh, raw):
        rows = slice(r * rb, (r + 1) * rb)
        lo, n_cur = band_cols(r)
        sc = raw[0] * scale + t_ref[hh, rows, bq:bq + n_cur]
        sp = raw[1] * scale + t_ref[hh, rows, lo:bq]
        sp = jnp.where(has_prev, sp, NEG_INF)
        mx = jnp.maximum(jnp.max(sc, axis=-1, keepdims=True), jnp.max(sp, axis=-1, keepdims=True))
        pc = jnp.exp(sc - mx)
        pp = jnp.exp(sp - mx)
        den = jnp.sum(pc, axis=-1, keepdims=True) + jnp.sum(pp, axis=-1, keepdims=True)
        return (_dot(pc.astype(BF16), vc_ref[0:n_cur, :]) + _dot(pp.astype(BF16), vp_ref[lo:bq, :])) / den

    items = [(r, hh) for r in range(bq // rb) for hh in range(2)]
    pending = [stage_logits(*it) for it in items[:BAND_AHEAD]]
    out_first = None
    for i, (r, hh) in enumerate(items):
        if i + BAND_AHEAD < len(items):
            pending.append(stage_logits(*items[i + BAND_AHEAD]))
        out = stage_softmax(r, hh, pending.pop(0))
        if hh == 0:
            out_first = out
        else:
            o_ref[r * rb:(r + 1) * rb, :] = jnp.where(first_half, out_first, out).astype(o_ref.dtype)


def _band_table(rel_bias, bq):
    band = BAND_CHUNKS * CHUNK
    n_g = 3 * bq - 1
    period = n_g + 1
    idx = np.clip(2 * bq - 1 - np.arange(n_g), -REL_CLIP, REL_CLIP) + REL_CLIP
    g = jnp.pad(rel_bias[:, idx].astype(F32), ((0, 0), (0, 1)))
    flat = jnp.tile(g, (1, bq + 1))[:, :bq * (period + 1)]
    hankel = flat.reshape(-1, bq, period + 1)[:, :, :2 * bq]
    toeplitz = hankel[:, ::-1, :]
    i = np.arange(bq)[:, None]
    j = np.arange(2 * bq)[None, :]
    first = (i // CHUNK) * CHUNK + bq - band
    in_band = (j >= first) & (j < (i // CHUNK) * CHUNK + bq + CHUNK)
    return jnp.where(jnp.asarray(in_band)[None], toeplitz, NEG_INF)


def _band_prompt(q, k, v, table, batch, seq, bq, rb):
    assert bq >= BAND_CHUNKS * CHUNK and rb % CHUNK == 0
    m, gw = q.shape
    nq = seq // bq
    cur = lambda p, b, i: (b * nq + i, p)
    prev = lambda p, b, i: (b * nq + jnp.maximum(i - 1, 0), p)
    blk = lambda f: pl.BlockSpec((bq, LANE), f)
    return pl.pallas_call(
        functools.partial(_band_kernel, scale=(LANE // 2) ** -0.5, bq=bq, rb=rb),
        grid=(A_HEADS // 2, batch, nq),
        in_specs=[blk(cur), blk(prev), blk(cur), blk(prev), blk(cur),
                  pl.BlockSpec((2, bq, 2 * bq), lambda p, b, i: (p, 0, 0))],
        out_specs=blk(cur),
        out_shape=jax.ShapeDtypeStruct((m, gw), BF16),
        compiler_params=_cparams("parallel", "parallel", "parallel"), name="band_prompt",
    )(q, k, k, v, v, table)


def _join_rows(cache, new, total):
    pad = total - cache.shape[0] - new.shape[0]
    parts = [cache.astype(BF16), new.astype(BF16)]
    if pad:
        parts.append(jnp.zeros((pad, cache.shape[1]), BF16))
    return jnp.concatenate(parts, axis=0)


def _band_sample_kernel(q_ref, kc_ref, vc_ref, kn_ref, vn_ref, t_ref, o_ref, *, scale, total):
    t_rows = q_ref.shape[0]
    first_half = lax.broadcasted_iota(jnp.int32, (t_rows, LANE), 1) < (LANE // 2)
    for p in range(A_HEADS // 2):
        cols = slice(p * LANE, (p + 1) * LANE)
        q = q_ref[:, cols]
        k = _join_rows(kc_ref[:, cols], kn_ref[:, cols], total)
        v = _join_rows(vc_ref[:, cols], vn_ref[:, cols], total)
        outs = []
        for hh in range(2):
            sel = first_half if hh == 0 else jnp.logical_not(first_half)
            qh = jnp.where(sel, q, jnp.zeros_like(q))
            s = _dot_nt(qh, k) * scale + t_ref[2 * p + hh]
            mx = jnp.max(s, axis=-1, keepdims=True)
            pr = jnp.exp(s - mx)
            outs.append(_dot(pr.astype(BF16), v) / jnp.sum(pr, axis=-1, keepdims=True))
        o_ref[:, cols] = jnp.where(first_half, outs[0], outs[1]).astype(o_ref.dtype)


def _softmax_sample_kernel(*refs, scale, fox, past, total, n_heads):
    if fox:
        q_ref, kc_ref, vc_ref, kn_ref, vn_ref, fq_ref, fk_ref, o_ref = refs
    else:
        q_ref, kc_ref, vc_ref, kn_ref, vn_ref, o_ref = refs
    t_rows = o_ref.shape[0]
    qpos = past + lax.broadcasted_iota(jnp.int32, (t_rows, total), 0)
    kpos = lax.broadcasted_iota(jnp.int32, (t_rows, total), 1)
    if fox:
        ok = kpos <= qpos
    else:
        ok = (kpos < past + t_rows) & (kpos // CHUNK <= qpos // CHUNK)
    for hh in range(n_heads):
        cols = slice(hh * LANE, (hh + 1) * LANE)
        if fox:
            q = q_ref[:, cols]
            k = _join_rows(kc_ref[:, cols], kn_ref[:, cols], total)
        else:
            q = q_ref[hh]
            k = _join_rows(kc_ref[hh], kn_ref[hh], total)
        v = _join_rows(vc_ref[:, cols], vn_ref[:, cols], total)
        s = _dot_nt(q, k) * scale
        if fox:
            s = s + fq_ref[:, hh:hh + 1] + fk_ref[hh:hh + 1, :]
        s = jnp.where(ok, s, NEG_INF)
        mx = jnp.max(s, axis=-1, keepdims=True)
        pr = jnp.exp(s - mx)
        o = _dot(pr.astype(BF16), v) / jnp.sum(pr, axis=-1, keepdims=True)
        o_ref[:, cols] = o.astype(o_ref.dtype)


def _sb_sample_kernel(q_ref, kc_ref, vc_ref, kn_ref, vn_ref, o_ref, *, scale, past, total, n_heads):
    t_rows = o_ref.shape[0]
    nb = total // LANE
    qpos = past + lax.broadcasted_iota(jnp.int32, (t_rows, total), 0)
    kpos = lax.broadcasted_iota(jnp.int32, (t_rows, total), 1)
    ok = kpos < qpos
    tri = _suffix_tri(LANE)
    for hh in range(n_heads):
        cols = slice(hh * LANE, (hh + 1) * LANE)
        k = _join_rows(kc_ref[:, cols], kn_ref[:, cols], total)
        v = _join_rows(vc_ref[:, cols], vn_ref[:, cols], total)
        z = _dot_nt(q_ref[:, cols], k) * scale
        log_beta = _log_sigmoid(z)
        log_fail = jnp.where(ok, log_beta - z, 0.0)
        stacked = jnp.concatenate([log_fail[:, c * LANE:(c + 1) * LANE] for c in range(nb)], axis=0)
        hi, lo = _split2(stacked)
        incl = _dot(hi, tri) + _dot(lo, tri)
        across = jnp.zeros((t_rows, 1), F32)
        w_blocks = [None] * nb
        for c in reversed(range(nb)):
            sl = slice(c * LANE, (c + 1) * LANE)
            inc = incl[c * t_rows:(c + 1) * t_rows]
            w = jnp.exp(log_beta[:, sl] + (inc - log_fail[:, sl] + across))
            w_blocks[c] = jnp.where(ok[:, sl], w, 0.0).astype(BF16)
            across = across + inc[:, 0:1]
        o_ref[:, cols] = _dot(jnp.concatenate(w_blocks, axis=-1), v).astype(o_ref.dtype)


def _sample_specs(n_streams, t_rows, past, gw):
    row = lambda b: (b, 0)
    return [pl.BlockSpec((t_rows, gw), row), pl.BlockSpec((past, gw), row), pl.BlockSpec((past, gw), row),
            pl.BlockSpec((t_rows, gw), row), pl.BlockSpec((t_rows, gw), row)]


def _pad_keys(n):
    return -(-n // LANE) * LANE


def _band_sample(q, kc, vc, kn, vn, table, n_streams):
    m, gw = q.shape
    t_rows, past = m // n_streams, kc.shape[0] // n_streams
    total = table.shape[-1]
    return pl.pallas_call(
        functools.partial(_band_sample_kernel, scale=(LANE // 2) ** -0.5, total=total),
        grid=(n_streams,),
        in_specs=_sample_specs(n_streams, t_rows, past, gw) + [_resident(table.shape)],
        out_specs=pl.BlockSpec((t_rows, gw), lambda b: (b, 0)),
        out_shape=jax.ShapeDtypeStruct((m, gw), BF16),
        compiler_params=_cparams("parallel"), name="band_sample",
    )(q, kc, vc, kn, vn, table)


def _fox_sample(q, kc, vc, kn, vn, f_rows, f_cols, n_streams):
    m, gw = q.shape
    t_rows, past = m // n_streams, kc.shape[0] // n_streams
    total = f_cols.shape[-1]
    return pl.pallas_call(
        functools.partial(_softmax_sample_kernel, scale=LANE ** -0.5, fox=True, past=past, total=total,
                          n_heads=L_HEADS),
        grid=(n_streams,),
        in_specs=_sample_specs(n_streams, t_rows, past, gw) + [
            pl.BlockSpec((t_rows, LANE), lambda b: (b, 0)),
            pl.BlockSpec((None, L_HEADS, total), lambda b: (b, 0, 0))],
        out_specs=pl.BlockSpec((t_rows, gw), lambda b: (b, 0)),
        out_shape=jax.ShapeDtypeStruct((m, gw), BF16),
        compiler_params=_cparams("parallel"), name="fox_sample",
    )(q, kc, vc, kn, vn, f_rows, f_cols)


def _sb_sample(q, kc, vc, kn, vn, n_streams):
    m, gw = q.shape
    t_rows, past = m // n_streams, kc.shape[0] // n_streams
    total = _pad_keys(past + t_rows)
    return pl.pallas_call(
        functools.partial(_sb_sample_kernel, scale=LANE ** -0.5, past=past, total=total, n_heads=L_HEADS),
        grid=(n_streams,),
        in_specs=_sample_specs(n_streams, t_rows, past, gw),
        out_specs=pl.BlockSpec((t_rows, gw), lambda b: (b, 0)),
        out_shape=jax.ShapeDtypeStruct((m, gw), BF16),
        compiler_params=_cparams("parallel"), name="sb_sample",
    )(q, kc, vc, kn, vn)


def _mla_sample(qd, kd_c, vd_c, kd_n, vd_n, n_streams):
    n_heads, m, dk = qd.shape
    gw = vd_n.shape[1]
    t_rows, past = m // n_streams, vd_c.shape[0] // n_streams
    total = _pad_keys(past + t_rows)
    lat = lambda rows: pl.BlockSpec((n_heads, rows, dk), lambda b: (0, b, 0))
    row = lambda b: (b, 0)
    return pl.pallas_call(
        functools.partial(_softmax_sample_kernel, scale=(NOPE_DIM + ROPE_DIM) ** -0.5, fox=False, past=past,
                          total=total, n_heads=n_heads),
        grid=(n_streams,),
        in_specs=[lat(t_rows), lat(past), pl.BlockSpec((past, gw), row), lat(t_rows),
                  pl.BlockSpec((t_rows, gw), row)],
        out_specs=pl.BlockSpec((t_rows, gw), row),
        out_shape=jax.ShapeDtypeStruct((m, gw), BF16),
        compiler_params=_cparams("parallel"), name="mla_sample",
    )(qd, kd_c, vd_c, kd_n, vd_n)


def _out_proj_kernel(oa_ref, ob_ref, oc_ref, od_ref, gmix_ref, wo_ref, x_ref, gpost_ref, gffn_ref,
                     x1_ref, h2_ref):
    gw = oa_ref.shape[1]
    y = None
    for gi, o_ref in enumerate((oa_ref, ob_ref, oc_ref, od_ref)):
        n = _rms(o_ref[...].astype(F32), gmix_ref[gi:gi + 1, :]).astype(BF16)
        part = _dot(n, wo_ref[gi * gw:(gi + 1) * gw, :])
        y = part if y is None else y + part
    x1 = x_ref[...] + _rms(y, gpost_ref[...])
    x1_ref[...] = x1
    h2_ref[...] = _rms(x1, gffn_ref[...]).astype(BF16)


def _out_proj(outs, g_mix, w_o, x, g_post, g_ffn, tm):
    m, d = x.shape
    gw = d // 4
    rows = lambda i: (i, 0)
    return pl.pallas_call(
        _out_proj_kernel, grid=(m // tm,),
        in_specs=[pl.BlockSpec((tm, gw), rows)] * 4 + [_resident(g_mix.shape), _resident(w_o.shape),
                                                       pl.BlockSpec((tm, d), rows),
                                                       _resident((1, d)), _resident((1, d))],
        out_specs=[pl.BlockSpec((tm, d), rows), pl.BlockSpec((tm, d), rows)],
        out_shape=[jax.ShapeDtypeStruct((m, d), F32), jax.ShapeDtypeStruct((m, d), BF16)],
        compiler_params=_cparams("parallel"), name="out_proj",
    )(*outs, g_mix, w_o, x, g_post, g_ffn)


def _ffn_kernel(h_ref, wg_ref, wu_ref, wd_ref, x_ref, g_ref, o_ref, acc_ref):
    f = pl.program_id(1)

    @pl.when(f == 0)
    def _():
        acc_ref[...] = jnp.zeros_like(acc_ref)

    h = h_ref[...]
    gate = _dot(h, wg_ref[...])
    up = _dot(h, wu_ref[...])
    act = (gate * jax.nn.sigmoid(gate) * up).astype(BF16)
    acc_ref[...] += _dot(act, wd_ref[...])

    @pl.when(f == pl.num_programs(1) - 1)
    def _():
        o_ref[...] = x_ref[...] + _rms(acc_ref[...], g_ref[...])


def _ffn(h2, w_gate, w_up, w_down, x1, g, tm, tf):
    m, d = x1.shape
    ff = w_gate.shape[1]
    rows = lambda i, f: (i, 0)
    return pl.pallas_call(
        _ffn_kernel, grid=(m // tm, ff // tf),
        in_specs=[pl.BlockSpec((tm, d), rows), pl.BlockSpec((d, tf), lambda i, f: (0, f)),
                  pl.BlockSpec((d, tf), lambda i, f: (0, f)), pl.BlockSpec((tf, d), lambda i, f: (f, 0)),
                  pl.BlockSpec((tm, d), rows), pl.BlockSpec((1, d), lambda i, f: (0, 0))],
        out_specs=pl.BlockSpec((tm, d), rows),
        out_shape=jax.ShapeDtypeStruct((m, d), F32),
        scratch_shapes=[pltpu.VMEM((tm, d), F32)],
        compiler_params=_cparams("parallel", "arbitrary"), name="ffn",
    )(h2, w_gate, w_up, w_down, x1, g)


def _pad_cols(w, n):
    return jnp.pad(w, ((0, 0), (0, n - w.shape[1])))


def _swap_halves(w):
    half = w.shape[1] // 2
    return jnp.concatenate([w[:, half:], w[:, :half]], axis=1)


def _layer_weights(l, d, w_in, b_f, g_q_lat, g_kv_lat, w_uq, w_uk, w_uv):
    gw = d // 4
    q_rank, kv_rank = g_q_lat.shape[1], g_kv_lat.shape[1]
    sizes = (3 * gw, 3 * gw, L_HEADS, 3 * gw, q_rank, kv_rank, ROPE_DIM)
    za, zb, zf, zc, dq, dkv, dkr = jnp.split(w_in[l], np.cumsum(sizes)[:-1].tolist(), axis=1)
    w_all = jnp.concatenate([za, zb, zc, dq, dkv, _pad_cols(dkr, LANE), _pad_cols(_swap_halves(dkr), LANE),
                             _pad_cols(zf, LANE)], axis=1).astype(BF16)
    hd = NOPE_DIM + ROPE_DIM
    uq = w_uq[l]
    nope = [uq[:, h * hd:h * hd + NOPE_DIM] for h in range(L_HEADS)]
    rope = [uq[:, h * hd + NOPE_DIM:(h + 1) * hd] for h in range(L_HEADS)]
    w_q = jnp.concatenate(nope + [_pad_cols(r, LANE) for r in rope], axis=1).astype(BF16)
    w_qs = jnp.concatenate([_pad_cols(_swap_halves(r), LANE) for r in rope], axis=1).astype(BF16)
    return dict(w_in=w_all, b_f=_pad_cols(b_f[l][None, :].astype(F32), LANE), g_q=g_q_lat[l][None, :],
                g_kv=g_kv_lat[l][None, :], w_q=w_q, w_qs=w_qs,
                w_uk=w_uk[l].astype(BF16), w_uv=w_uv[l].astype(BF16))


def _rope_tables(pos):
    half = ROPE_DIM // 2
    inv_freq = ROPE_THETA ** (-jnp.arange(half, dtype=F32) / half)
    ang = pos.astype(F32)[:, None] * inv_freq[None, :]
    cos, sin = jnp.cos(ang), jnp.sin(ang)
    zero = jnp.zeros((pos.shape[0], LANE - ROPE_DIM), F32)
    return jnp.concatenate([cos, cos, zero], axis=1), jnp.concatenate([-sin, sin, zero], axis=1)


def kernel(x_prompt, x_sample, cache_a_k, cache_a_v, cache_b_k, cache_b_v, cache_b_logf, cache_c_k, cache_c_v,
           cache_d_ckv, cache_d_kpe, w_in, b_f, rel_bias, g_q_lat, g_kv_lat, w_uq, w_uk, w_uv, g_mix, w_o,
           g_pre_attn, g_post_attn, g_pre_ffn, g_post_ffn, w_gate, w_up, w_down):
    batch, seq, d = x_prompt.shape
    n_streams, t_rows, _ = x_sample.shape
    depth = w_in.shape[0]
    past = cache_b_k.shape[2]
    a_win = cache_a_k.shape[2]
    gw = d // 4
    mp, ms = batch * seq, n_streams * t_rows
    bq, rb = 512, 128
    tm_p, tm_s = 256, ms

    cos_p, sin_p = _rope_tables(jnp.arange(seq))
    cos_s, sin_s = _rope_tables(jnp.tile(past + jnp.arange(t_rows), n_streams))

    xp = x_prompt.reshape(mp, d)
    xs = x_sample.reshape(ms, d)
    p_states, s_states = [], []
    for l in range(depth):
        lw = _layer_weights(l, d, w_in, b_f, g_q_lat, g_kv_lat, w_uq, w_uk, w_uv)
        g_attn = g_pre_attn[l][None, :]
        wo = w_o[l].astype(BF16)
        wg, wu, wd = w_gate[l].astype(BF16), w_up[l].astype(BF16), w_down[l].astype(BF16)
        g_post, g_ffn, g_out = g_post_attn[l][None, :], g_pre_ffn[l][None, :], g_post_ffn[l][None, :]

        pr = _project(xp, g_attn, lw, cos_p, sin_p, tm_p)
        f_run = _cumsum(pr['logf'], seq)
        f_cols = (-f_run[:, :L_HEADS]).reshape(batch, seq, L_HEADS).transpose(0, 2, 1)
        f_cols = f_cols.reshape(batch, L_HEADS, seq // bq, bq)
        bq_flash = 2 * bq if seq % (2 * bq) == 0 else bq
        kd, vd = _mla_keys(pr['ckv'], pr['kpe128'], lw['w_uk'], lw['w_uv'], 512)
        oa = _band_prompt(pr['qa'], pr['ka16'], pr['va16'], _band_table(rel_bias[l], bq), batch, seq, bq, rb)
        ob = _fox_prompt(pr['qb'], pr['kb16'], pr['vb16'], f_run, f_cols, batch, seq, bq_flash, bq, rb)
        oc = _sb_prompt(pr['qc'], pr['kc16'], pr['vc16'], batch, seq, bq, rb)
        od = _mla_prompt(pr['qd'], kd, vd, batch, seq, bq_flash, bq, rb)
        x1, h2 = _out_proj((oa, ob, oc, od), g_mix[l], wo, xp, g_post, g_ffn, tm_p)
        xp = _ffn(h2, wg, wu, wd, x1, g_out, 512, 512)
        keep = min(a_win, seq)
        p_states.append((
            pr['ka'].reshape(batch, seq, A_HEADS, gw // A_HEADS)[:, seq - keep:],
            pr['va'].reshape(batch, seq, A_HEADS, gw // A_HEADS)[:, seq - keep:],
            pr['kb'].reshape(batch, seq, L_HEADS, LANE), pr['vb'].reshape(batch, seq, L_HEADS, LANE),
            pr['logf'][:, :L_HEADS].reshape(batch, seq, L_HEADS),
            pr['kc'].reshape(batch, seq, L_HEADS, LANE), pr['vc'].reshape(batch, seq, L_HEADS, LANE),
            pr['ckv'].reshape(batch, seq, -1), pr['kpe'].reshape(batch, seq, ROPE_DIM)))

        sr = _project(xs, g_attn, lw, cos_s, sin_s, tm_s)
        qa, ka, va, qb, kb, vb, qc, kc, vc = (sr[n] for n in ('qa', 'ka', 'va', 'qb', 'kb', 'vb', 'qc', 'kc', 'vc'))
        logf, qd, ckv, kpe, kpe128 = (sr[n] for n in ('logf', 'qd', 'ckv', 'kpe', 'kpe128'))
        tot_a = _pad_keys(a_win + t_rows)
        rel = (a_win + np.arange(t_rows))[:, None] - np.arange(tot_a)[None, :]
        tab = rel_bias[l][:, np.clip(rel, -REL_CLIP, REL_CLIP) + REL_CLIP].astype(F32)
        tab = jnp.where(jnp.asarray(np.arange(tot_a) < a_win + t_rows)[None, None, :], tab, NEG_INF)
        oa = _band_sample(qa, cache_a_k[l].reshape(n_streams * a_win, gw),
                          cache_a_v[l].reshape(n_streams * a_win, gw), ka, va, tab, n_streams)
        f_new = _cumsum(logf, t_rows)
        cl = _pad_cols(cache_b_logf[l].reshape(n_streams * past, L_HEADS).astype(F32), LANE)
        suffix = _cumsum(cl, past, reverse=True)
        tot = _pad_keys(past + t_rows)
        f_cols = jnp.concatenate([suffix[:, :L_HEADS].reshape(n_streams, past, L_HEADS),
                                  -f_new[:, :L_HEADS].reshape(n_streams, t_rows, L_HEADS),
                                  jnp.zeros((n_streams, tot - past - t_rows, L_HEADS), F32)], axis=1)
        f_cols = f_cols.transpose(0, 2, 1)
        ob = _fox_sample(qb, cache_b_k[l].reshape(n_streams * past, gw),
                         cache_b_v[l].reshape(n_streams * past, gw), kb, vb, f_new, f_cols, n_streams)
        oc = _sb_sample(qc, cache_c_k[l].reshape(n_streams * past, gw),
                        cache_c_v[l].reshape(n_streams * past, gw), kc, vc, n_streams)
        kd_n, vd_n = _mla_keys(ckv, kpe128, lw['w_uk'], lw['w_uv'], ms)
        kd_c, vd_c = _mla_keys(cache_d_ckv[l].reshape(n_streams * past, -1),
                               _pad_cols(cache_d_kpe[l].reshape(n_streams * past, ROPE_DIM), LANE),
                               lw['w_uk'], lw['w_uv'], 512)
        od = _mla_sample(qd, kd_c, vd_c, kd_n, vd_n, n_streams)
        x1, h2 = _out_proj((oa, ob, oc, od), g_mix[l], wo, xs, g_post, g_ffn, tm_s)
        xs = _ffn(h2, wg, wu, wd, x1, g_out, ms, 512)
        s_states.append((
            ka.reshape(n_streams, t_rows, A_HEADS, gw // A_HEADS),
            va.reshape(n_streams, t_rows, A_HEADS, gw // A_HEADS),
            kb.reshape(n_streams, t_rows, L_HEADS, LANE), vb.reshape(n_streams, t_rows, L_HEADS, LANE),
            logf[:, :L_HEADS].reshape(n_streams, t_rows, L_HEADS),
            kc.reshape(n_streams, t_rows, L_HEADS, LANE), vc.reshape(n_streams, t_rows, L_HEADS, LANE),
            ckv.reshape(n_streams, t_rows, -1), kpe.reshape(n_streams, t_rows, ROPE_DIM)))

    p_out = [jnp.stack(s) for s in zip(*p_states)]
    s_out = [jnp.stack(s) for s in zip(*s_states)]
    return (xp.reshape(batch, seq, d), xs.reshape(n_streams, t_rows, d), *p_out, *s_out)
```

```python
import functools
import math

import numpy as np
import jax
import jax.numpy as jnp
from jax import lax
from jax.experimental import pallas as pl
from jax.experimental.pallas import tpu as pltpu

CHUNK = 64
A_HEADS = 8
L_HEADS = 4
BAND_CHUNKS = 8
REL_CLIP = 128
NOPE_DIM = 128
ROPE_DIM = 64
ROPE_THETA = 10000.0
RMS_EPS = 1e-6
NEG_INF = -1e30
LOG2E = math.log2(math.e)

LANE = 128
MXU_DEPTH = 256
VMEM_LIMIT = 56 * 1024 * 1024

F32 = jnp.float32
BF16 = jnp.bfloat16


def _cparams(*sem):
    return pltpu.CompilerParams(dimension_semantics=sem, vmem_limit_bytes=VMEM_LIMIT)


def _resident(shape):
    return pl.BlockSpec(shape, lambda *_: (0,) * len(shape), pipeline_mode=pl.Buffered(1))


def _rms(x, g):
    return x * lax.rsqrt(jnp.mean(x * x, axis=-1, keepdims=True) + RMS_EPS) * g


def _log_sigmoid(x):
    return jnp.minimum(x, 0.0) - jnp.log1p(jnp.exp(-jnp.abs(x)))


def _dot(a, b):
    return jnp.dot(a, b, preferred_element_type=F32)


def _dot_nt(a, b):
    return lax.dot_general(a, b, (((1,), (1,)), ((), ())), preferred_element_type=F32)


def _split2(x):
    hi = x.astype(BF16)
    lo = (x - hi.astype(F32)).astype(BF16)
    return hi, lo


def _split3(x):
    hi = x.astype(BF16)
    r = x - hi.astype(F32)
    mid = r.astype(BF16)
    lo = (r - mid.astype(F32)).astype(BF16)
    return hi, mid, lo


_PROJ_INPUTS = 10


def _proj_kernel(*refs, gw, q_rank, kv_rank, rope_dim, q_scales):
    x_ref, g_ref, w_ref, bf_ref, gq_ref, gkv_ref, wq_ref, wqs_ref, cos_ref, sin_ref = refs[:_PROJ_INPUTS]
    (qa_ref, ka_ref, va_ref, ka16_ref, va16_ref, qb_ref, kb_ref, vb_ref, kb16_ref, vb16_ref,
     qc_ref, kc_ref, vc_ref, kc16_ref, vc16_ref, logf_ref, qd_ref, ckv_ref, kpe_ref, kpe128_ref,
     ) = refs[len(refs) - len(_PROJ_OUTPUTS):]
    h = _rms(x_ref[...], g_ref[...]).astype(BF16)

    def seg(a, n):
        return _dot(h, w_ref[:, a:a + n])

    groups = ((qa_ref, ka_ref, va_ref, ka16_ref, va16_ref), (qb_ref, kb_ref, vb_ref, kb16_ref, vb16_ref),
              (qc_ref, kc_ref, vc_ref, kc16_ref, vc16_ref))
    for gi, (q_r, k_r, v_r, k16_r, v16_r) in enumerate(groups):
        base = 3 * gw * gi
        q_r[...] = (seg(base, gw) * q_scales[gi]).astype(BF16)
        k = seg(base + gw, gw)
        k_r[...] = k
        k16_r[...] = k.astype(BF16)
        v = seg(base + 2 * gw, gw)
        v_r[...] = v
        v16_r[...] = v.astype(BF16)
    off = 9 * gw
    dq = seg(off, q_rank)
    off += q_rank
    dkv = seg(off, kv_rank)
    off += kv_rank
    dkr = seg(off, LANE)
    dkr_sw = seg(off + LANE, LANE)
    zf = seg(off + 2 * LANE, LANE)

    logf_ref[...] = _log_sigmoid(zf + bf_ref[...])
    ckv_ref[...] = _rms(dkv, gkv_ref[...])
    cos = cos_ref[...]
    sin = sin_ref[...]
    kpe = dkr * cos + dkr_sw * sin
    kpe128_ref[...] = kpe
    kpe_ref[...] = kpe[:, :rope_dim]

    nq = _rms(dq, gq_ref[...]).astype(BF16)
    qf = _dot(nq, wq_ref[...])
    qs = _dot(nq, wqs_ref[...])
    n_heads = qd_ref.shape[0]
    for hh in range(n_heads):
        qd_ref[hh, :, 0:LANE] = (qf[:, LANE * hh:LANE * (hh + 1)] * q_scales[3]).astype(BF16)
        r0 = LANE * (n_heads + hh)
        rot = qf[:, r0:r0 + LANE] * cos + qs[:, LANE * hh:LANE * (hh + 1)] * sin
        qd_ref[hh, :, LANE:2 * LANE] = (rot * q_scales[3]).astype(BF16)


def _q_scales(gw):
    return ((gw // A_HEADS) ** -0.5 * LOG2E, (gw // L_HEADS) ** -0.5 * LOG2E, (gw // L_HEADS) ** -0.5,
            (NOPE_DIM + ROPE_DIM) ** -0.5 * LOG2E)


_PROJ_OUTPUTS = ('qa', 'ka', 'va', 'ka16', 'va16', 'qb', 'kb', 'vb', 'kb16', 'vb16',
                 'qc', 'kc', 'vc', 'kc16', 'vc16', 'logf', 'qd', 'ckv', 'kpe', 'kpe128')


_STACKED_STATES = ('kb', 'vb', 'kc', 'vc')


def _project(x, g, lw, cos_t, sin_t, tm, stack=None):
    m, d = x.shape
    gw = d // 4
    npad = lw['w_in'].shape[1]
    q_rank, kv_rank = lw['g_q'].shape[1], lw['g_kv'].shape[1]
    n_pos_blocks = cos_t.shape[0] // tm
    rows = lambda i: (i, 0)
    pos = lambda i: (i % n_pos_blocks, 0)
    f32o = lambda n: jax.ShapeDtypeStruct((m, n), F32)
    bf16o = lambda n: jax.ShapeDtypeStruct((m, n), BF16)
    blk = lambda n: pl.BlockSpec((tm, n), rows)
    group = [bf16o(gw), f32o(gw), f32o(gw), bf16o(gw), bf16o(gw)]
    out_shape = group * 3 + [f32o(LANE), jax.ShapeDtypeStruct((L_HEADS, m, 2 * LANE), BF16),
                             f32o(kv_rank), f32o(ROPE_DIM), f32o(LANE)]
    out_specs = [blk(gw)] * 15 + [blk(LANE), pl.BlockSpec((L_HEADS, tm, 2 * LANE), lambda i: (0, i, 0)),
                                  blk(kv_rank), blk(ROPE_DIM), blk(LANE)]
    in_specs = [blk(d), _resident((1, d)), _resident((d, npad)), _resident((1, LANE)),
                _resident((1, q_rank)), _resident((1, kv_rank)),
                _resident(lw['w_q'].shape), _resident(lw['w_qs'].shape),
                pl.BlockSpec((tm, LANE), pos), pl.BlockSpec((tm, LANE), pos)]
    args = [x, g, lw['w_in'], lw['b_f'], lw['g_q'], lw['g_kv'], lw['w_q'], lw['w_qs'], cos_t, sin_t]
    aliases = {}
    if stack is not None:
        layer, depth, buffers = stack
        for name in _STACKED_STATES:
            o = _PROJ_OUTPUTS.index(name)
            out_shape[o] = jax.ShapeDtypeStruct((depth, m, gw), F32)
            out_specs[o] = pl.BlockSpec((None, tm, gw), lambda i: (layer, i, 0))
            if buffers is not None:
                aliases[len(args)] = o
                args.append(buffers[name])
                in_specs.append(pl.BlockSpec(memory_space=pl.ANY))
    kern = functools.partial(_proj_kernel, gw=gw, q_rank=q_rank, kv_rank=kv_rank, rope_dim=ROPE_DIM,
                             q_scales=_q_scales(gw))
    outs = pl.pallas_call(
        kern, grid=(m // tm,), in_specs=in_specs, out_specs=out_specs, out_shape=out_shape,
        input_output_aliases=aliases, compiler_params=_cparams("parallel"), name="in_proj",
    )(*args)
    return dict(zip(_PROJ_OUTPUTS, outs))


def _mla_keys_kernel(ckv_ref, kpe_ref, wuk_ref, wuv_ref, kd_ref, vd_ref):
    c = ckv_ref[...].astype(BF16)
    kn = _dot(c, wuk_ref[...])
    vd_ref[...] = _dot(c, wuv_ref[...]).astype(BF16)
    kp = kpe_ref[...].astype(BF16)
    for hh in range(kd_ref.shape[0]):
        kd_ref[hh, :, 0:LANE] = kn[:, LANE * hh:LANE * (hh + 1)].astype(BF16)
        kd_ref[hh, :, LANE:2 * LANE] = kp


def _mla_keys(ckv, kpe128, w_uk, w_uv, tm):
    m, r = ckv.shape
    n = w_uk.shape[1]
    rows = lambda i: (i, 0)
    return pl.pallas_call(
        _mla_keys_kernel, grid=(m // tm,),
        in_specs=[pl.BlockSpec((tm, r), rows), pl.BlockSpec((tm, LANE), rows),
                  _resident(w_uk.shape), _resident(w_uv.shape)],
        out_specs=[pl.BlockSpec((L_HEADS, tm, 2 * LANE), lambda i: (0, i, 0)), pl.BlockSpec((tm, n), rows)],
        out_shape=[jax.ShapeDtypeStruct((L_HEADS, m, 2 * LANE), BF16), jax.ShapeDtypeStruct((m, n), BF16)],
        compiler_params=_cparams("parallel"), name="mla_keys",
    )(ckv, kpe128, w_uk, w_uv)


def _cumsum_kernel(x_ref, o_ref, carry_ref, *, seg, reverse):
    @pl.when(pl.program_id(1) == 0)
    def _():
        carry_ref[...] = jnp.zeros_like(carry_ref)

    x = x_ref[...]
    tc = x.shape[0]
    r = lax.broadcasted_iota(jnp.int32, (tc, tc), 0)
    c = lax.broadcasted_iota(jnp.int32, (tc, tc), 1)
    keep = (c > r) if reverse else (c <= r)
    if seg < tc:
        keep = keep & (r // seg == c // seg)
    tri = jnp.where(keep, 1.0, 0.0).astype(BF16)
    hi, mid, lo = _split3(x)
    y = _dot(tri, hi) + _dot(tri, mid) + _dot(tri, lo)
    o_ref[...] = y + carry_ref[...]
    if seg > tc:
        carry_ref[...] += jnp.sum(x, axis=0, keepdims=True)


def _cumsum(x, seg, reverse=False):
    m, n = x.shape
    tc = min(512, m) if seg >= 512 else min(256, m)
    nb = max(seg // tc, 1)
    nseg = m // (nb * tc)
    if reverse:
        idx = lambda s, j: (s * nb + nb - 1 - j, 0)
    else:
        idx = lambda s, j: (s * nb + j, 0)
    return pl.pallas_call(
        functools.partial(_cumsum_kernel, seg=seg, reverse=reverse), grid=(nseg, nb),
        in_specs=[pl.BlockSpec((tc, n), idx)], out_specs=pl.BlockSpec((tc, n), idx),
        out_shape=jax.ShapeDtypeStruct((m, n), F32),
        scratch_shapes=[pltpu.VMEM((1, n), F32)],
        compiler_params=_cparams("parallel", "arbitrary"), name="cumsum",
    )(x)


def _lane_tile(x, n):
    return x if n == LANE else jnp.concatenate([x] * (n // LANE), axis=1)


FLASH_AHEAD = 4


def _flash_kernel(*refs, fox, bq, bk, rb):
    if fox:
        q_ref, k_ref, v_ref, fq_ref, fk_ref, o_ref, acc_ref, m_ref, l_ref = refs
    else:
        q_ref, k_ref, v_ref, o_ref, acc_ref, m_ref, l_ref = refs
    head = pl.program_id(1)
    qi = pl.program_id(2)
    m_ref[...] = jnp.full_like(m_ref, NEG_INF)
    l_ref[...] = jnp.zeros_like(l_ref)
    acc_ref[...] = jnp.zeros_like(acc_ref)
    if fox:
        lane = lax.broadcasted_iota(jnp.int32, fq_ref.shape, 1)
        fq = jnp.sum(jnp.where(lane == head, fq_ref[...], 0.0), axis=-1, keepdims=True) * LOG2E

    def logits(r, j):
        rows = slice(r * rb, (r + 1) * rb)
        s = _dot_nt(q_ref[rows, :], k_ref[pl.ds(pl.multiple_of(j * bk, bk), bk), :])
        if fox:
            s = fq[rows] + (fk_ref[pl.ds(j, 1), :] * LOG2E + s)
        return s

    def update(r, s, j):
        rows = slice(r * rb, (r + 1) * rb)
        m_prev = m_ref[rows, :]
        m_new = jnp.maximum(m_prev, jnp.max(s, axis=-1, keepdims=True))
        alpha = jnp.exp2(m_prev - m_new)
        p = jnp.exp2(s - _lane_tile(m_new, s.shape[1]))
        l_ref[rows, :] = alpha * l_ref[rows, :] + jnp.sum(p, axis=-1, keepdims=True)
        v = v_ref[pl.ds(pl.multiple_of(j * bk, bk), bk), :]
        acc_ref[rows, :] = alpha * acc_ref[rows, :] + _dot(p.astype(BF16), v)
        m_ref[rows, :] = m_new

    def run(items):
        pending = [logits(r, j) for r, j, _ in items[:FLASH_AHEAD]]
        for i, (r, j, diag) in enumerate(items):
            if i + FLASH_AHEAD < len(items):
                pending.append(logits(*items[i + FLASH_AHEAD][:2]))
            s = pending.pop(0)
            if diag is not None:
                row = r * rb + lax.broadcasted_iota(jnp.int32, s.shape, 0)
                col = diag + lax.broadcasted_iota(jnp.int32, s.shape, 1)
                ok = (col <= row) if fox else (col // CHUNK <= row // CHUNK)
                s = jnp.where(ok, s, NEG_INF)
            update(r, s, j)

    n_groups = bq // rb
    kb_per_q = bq // bk

    def full_block(j, carry):
        run([(r, j, None) for r in range(n_groups)])
        return carry

    lax.fori_loop(0, qi * kb_per_q, full_block, 0)

    items = []
    for kb in range(kb_per_q):
        first = kb * bk // rb
        items += [(r, qi * kb_per_q + kb, kb * bk) for r in range(first, first + bk // rb)]
        items += [(r, qi * kb_per_q + kb, None) for r in range(first + bk // rb, n_groups)]
    run(items)
    o_ref[...] = (acc_ref[...] / l_ref[...]).astype(o_ref.dtype)


def _flash_scratch(bq):
    return [pltpu.VMEM((bq, LANE), F32), pltpu.VMEM((bq, LANE), F32), pltpu.VMEM((bq, LANE), F32)]


def _fox_prompt(q, k, v, f_rows, f_cols, batch, seq, bq, bk, rb):
    m, gw = q.shape
    nq = seq // bq
    qmap = lambda b, h, i: (b * nq + i, h)
    kvmap = lambda b, h, i: (b, h)
    return pl.pallas_call(
        functools.partial(_flash_kernel, fox=True, bq=bq, bk=bk, rb=rb),
        grid=(batch, L_HEADS, nq),
        in_specs=[pl.BlockSpec((bq, LANE), qmap), pl.BlockSpec((seq, LANE), kvmap),
                  pl.BlockSpec((seq, LANE), kvmap),
                  pl.BlockSpec((bq, LANE), lambda b, h, i: (b * nq + i, 0)),
                  pl.BlockSpec((None, None, seq // bk, bk), lambda b, h, i: (b, h, 0, 0))],
        out_specs=pl.BlockSpec((bq, LANE), qmap),
        out_shape=jax.ShapeDtypeStruct((m, gw), BF16),
        scratch_shapes=_flash_scratch(bq),
        compiler_params=_cparams("parallel", "parallel", "arbitrary"), name="fox_prompt",
    )(q, k, v, f_rows, f_cols)


def _mla_prompt(qd, kd, vd, batch, seq, bq, bk, rb):
    n_heads, m, dk = qd.shape
    nq = seq // bq
    return pl.pallas_call(
        functools.partial(_flash_kernel, fox=False, bq=bq, bk=bk, rb=rb),
        grid=(batch, n_heads, nq),
        in_specs=[pl.BlockSpec((None, bq, dk), lambda b, h, i: (h, b * nq + i, 0)),
                  pl.BlockSpec((None, seq, dk), lambda b, h, i: (h, b, 0)),
                  pl.BlockSpec((seq, LANE), lambda b, h, i: (b, h))],
        out_specs=pl.BlockSpec((bq, LANE), lambda b, h, i: (b * nq + i, h)),
        out_shape=jax.ShapeDtypeStruct((m, n_heads * LANE), BF16),
        scratch_shapes=_flash_scratch(bq),
        compiler_params=_cparams("parallel", "parallel", "arbitrary"), name="mla_prompt",
    )(qd, kd, vd)


SB_UNDERFLOW = 105.0


def _suffix_tri(n):
    r = lax.broadcasted_iota(jnp.int32, (n, n), 0)
    c = lax.broadcasted_iota(jnp.int32, (n, n), 1)
    return jnp.where(r >= c, 1.0, 0.0).astype(BF16)


def _sb_kernel(q_ref, k_ref, v_ref, o_ref, acc_ref, across_ref, *, bq, rb, sub):
    qi = pl.program_id(2)
    acc_ref[...] = jnp.zeros_like(acc_ref)
    across_ref[...] = jnp.zeros_like(across_ref)
    tri = _suffix_tri(sub)

    n_groups = bq // rb

    def visit(start, diagonal):
        n_keys = [-(-((r + 1) * rb) // sub) * sub if diagonal else bq for r in range(n_groups)]

        def stage_logits(r):
            rows = slice(r * rb, (r + 1) * rb)
            return _dot_nt(q_ref[rows, :], k_ref[pl.ds(start, n_keys[r]), :])

        def stage_sums(r, z):
            log_fail = -jnp.maximum(z, 0.0) - jnp.log(1.0 + jnp.exp(-jnp.abs(z)))
            log_beta = log_fail + z
            ok = None
            if diagonal:
                row = r * rb + lax.broadcasted_iota(jnp.int32, z.shape, 0)
                col = lax.broadcasted_iota(jnp.int32, z.shape, 1)
                ok = col < row
                log_fail = jnp.where(ok, log_fail, 0.0)
            incl = {}
            for c in reversed(range(n_keys[r] // sub)):
                hi, lo = _split2(log_fail[:, c * sub:(c + 1) * sub])
                incl[c] = _dot(hi, tri) + _dot(lo, tri)
            return log_fail, log_beta, ok, incl

        def stage_values(r, state):
            log_fail, log_beta, ok, incl = state
            rows = slice(r * rb, (r + 1) * rb)
            across = across_ref[rows, :]
            acc = acc_ref[rows, :]
            for c in reversed(range(n_keys[r] // sub)):
                sl = slice(c * sub, (c + 1) * sub)
                w = jnp.exp(log_beta[:, sl] + (incl[c] - log_fail[:, sl] + _lane_tile(across, sub)))
                if diagonal:
                    w = jnp.where(ok[:, sl], w, 0.0)
                vb = v_ref[pl.ds(pl.multiple_of(start + c * sub, sub), sub), :]
                acc = acc + _dot(w.astype(BF16), vb)
                across = across + jnp.broadcast_to(incl[c][:, 0:1], across.shape)
            across_ref[rows, :] = across
            acc_ref[rows, :] = acc

        zs = [stage_logits(r) for r in range(n_groups)]
        states = [stage_sums(r, zs[r]) for r in range(min(2, n_groups))]
        for r in range(n_groups):
            stage_values(r, states[r])
            if r + 2 < n_groups:
                states.append(stage_sums(r + 2, zs[r + 2]))

    visit(pl.multiple_of(qi * bq, bq), True)

    def more(carry):
        t, worst = carry
        return jnp.logical_and(t < qi, worst > -SB_UNDERFLOW)

    def earlier_block(carry):
        t, _ = carry
        visit(pl.multiple_of((qi - 1 - t) * bq, bq), False)
        return t + 1, jnp.max(across_ref[...])

    lax.while_loop(more, earlier_block, (jnp.int32(0), jnp.max(across_ref[...])))
    o_ref[...] = acc_ref[...].astype(o_ref.dtype)


def _sb_prompt(q, k, v, batch, seq, bq, rb):
    m, gw = q.shape
    nq = seq // bq
    qmap = lambda b, h, i: (b * nq + i, h)
    kvmap = lambda b, h, i: (b, h)
    return pl.pallas_call(
        functools.partial(_sb_kernel, bq=bq, rb=rb, sub=MXU_DEPTH),
        grid=(batch, L_HEADS, nq),
        in_specs=[pl.BlockSpec((bq, LANE), qmap), pl.BlockSpec((seq, LANE), kvmap),
                  pl.BlockSpec((seq, LANE), kvmap)],
        out_specs=pl.BlockSpec((bq, LANE), qmap),
        out_shape=jax.ShapeDtypeStruct((m, gw), BF16),
        scratch_shapes=[pltpu.VMEM((bq, LANE), F32), pltpu.VMEM((bq, LANE), F32)],
        compiler_params=_cparams("parallel", "parallel", "arbitrary"), name="sb_prompt",
    )(q, k, v)


BAND_AHEAD = 2


def _band_kernel(q_ref, kp_ref, kc_ref, vp_ref, vc_ref, t_ref, o_ref, *, bq, rb):
    band = BAND_CHUNKS * CHUNK
    has_prev = pl.program_id(2) > 0
    first_half = lax.broadcasted_iota(jnp.int32, (rb, LANE), 1) < (LANE // 2)

    def band_cols(r):
        lo = bq - band + r * rb
        n_cur = (r + 1) * rb
        return lo, n_cur

    def stage_logits(r, hh):
        rows = slice(r * rb, (r + 1) * rb)
        lo, n_cur = band_cols(r)
        q = q_ref[rows, :]
        sel = first_half if hh == 0 else jnp.logical_not(first_half)
        qh = jnp.where(sel, q, jnp.zeros_like(q))
        return _dot_nt(qh, kc_ref[0:n_cur, :]), _dot_nt(qh, kp_ref[lo:bq, :])

    def stage_softmax(r, hh, raw):
        lo, n_cur = band_cols(r)
        n_prev = bq - lo
        sc = t_ref[hh, :, n_prev:n_prev + n_cur] + raw[0]
        sp = t_ref[hh, :, 0:n_prev] + raw[1]
        sp = jnp.where(has_prev, sp, NEG_INF)
        mx = jnp.maximum(jnp.max(sc, axis=-1, keepdims=True), jnp.max(sp, axis=-1, keepdims=True))
        pc = jnp.exp2(sc - mx)
        pp = jnp.exp2(sp - mx)
        den = jnp.sum(pc, axis=-1, keepdims=True) + jnp.sum(pp, axis=-1, keepdims=True)
        return (_dot(pc.astype(BF16), vc_ref[0:n_cur, :]) + _dot(pp.astype(BF16), vp_ref[lo:bq, :])) / den

    items = [(r, hh) for r in range(bq // rb) for hh in range(2)]
    pending = [stage_logits(*it) for it in items[:BAND_AHEAD]]
    out_first = None
    for i, (r, hh) in enumerate(items):
        if i + BAND_AHEAD < len(items):
            pending.append(stage_logits(*items[i + BAND_AHEAD]))
        out = stage_softmax(r, hh, pending.pop(0))
        if hh == 0:
            out_first = out
        else:
            o_ref[r * rb:(r + 1) * rb, :] = jnp.where(first_half, out_first, out).astype(o_ref.dtype)


def _band_table_kernel(g_ref, o_ref):
    rb, width = o_ref.shape
    x = jnp.broadcast_to(g_ref[...], (rb, g_ref.shape[-1]))
    y = pltpu.roll(x, 0, 1, stride=1, stride_axis=0)[:, rb:]
    i = lax.broadcasted_iota(jnp.int32, (rb, width), 0)
    j = lax.broadcasted_iota(jnp.int32, (rb, width), 1)
    first = (i // CHUNK) * CHUNK
    ok = (j >= first) & (j < first + (BAND_CHUNKS + 1) * CHUNK)
    o_ref[...] = jnp.where(ok, y * LOG2E, NEG_INF)


def _band_table(rel_bias, rb):
    band = BAND_CHUNKS * CHUNK
    width = band + rb
    rel = band + rb - np.arange(rb + width)
    g = rel_bias[:, np.clip(rel, -REL_CLIP, REL_CLIP) + REL_CLIP].astype(F32)[:, None, :]
    n_heads = rel_bias.shape[0]
    return pl.pallas_call(
        _band_table_kernel, grid=(n_heads,),
        in_specs=[pl.BlockSpec((None, 1, rb + width), lambda h: (h, 0, 0))],
        out_specs=pl.BlockSpec((None, rb, width), lambda h: (h, 0, 0)),
        out_shape=jax.ShapeDtypeStruct((n_heads, rb, width), F32),
        compiler_params=_cparams("parallel"), name="band_table",
    )(g)


def _band_prompt(q, k, v, table, batch, seq, bq, rb):
    assert bq >= BAND_CHUNKS * CHUNK and rb % CHUNK == 0
    m, gw = q.shape
    nq = seq // bq
    cur = lambda p, b, i: (b * nq + i, p)
    prev = lambda p, b, i: (b * nq + jnp.maximum(i - 1, 0), p)
    blk = lambda f: pl.BlockSpec((bq, LANE), f)
    return pl.pallas_call(
        functools.partial(_band_kernel, bq=bq, rb=rb),
        grid=(A_HEADS // 2, batch, nq),
        in_specs=[blk(cur), blk(prev), blk(cur), blk(prev), blk(cur),
                  pl.BlockSpec((2,) + table.shape[1:], lambda p, b, i: (p, 0, 0))],
        out_specs=blk(cur),
        out_shape=jax.ShapeDtypeStruct((m, gw), BF16),
        compiler_params=_cparams("parallel", "parallel", "parallel"), name="band_prompt",
    )(q, k, k, v, v, table)


def _join_rows(cache, new, total):
    pad = total - cache.shape[0] - new.shape[0]
    parts = [cache.astype(BF16), new.astype(BF16)]
    if pad:
        parts.append(jnp.zeros((pad, cache.shape[1]), BF16))
    return jnp.concatenate(parts, axis=0)


def _band_sample_kernel(q_ref, kc_ref, vc_ref, kn_ref, vn_ref, t_ref, o_ref, *, total):
    t_rows = q_ref.shape[0]
    first_half = lax.broadcasted_iota(jnp.int32, (t_rows, LANE), 1) < (LANE // 2)
    for p in range(A_HEADS // 2):
        cols = slice(p * LANE, (p + 1) * LANE)
        q = q_ref[:, cols]
        k = _join_rows(kc_ref[:, cols], kn_ref[:, cols], total)
        v = _join_rows(vc_ref[:, cols], vn_ref[:, cols], total)
        outs = []
        for hh in range(2):
            sel = first_half if hh == 0 else jnp.logical_not(first_half)
            qh = jnp.where(sel, q, jnp.zeros_like(q))
            s = t_ref[2 * p + hh] * LOG2E + _dot_nt(qh, k)
            mx = jnp.max(s, axis=-1, keepdims=True)
            pr = jnp.exp2(s - mx)
            outs.append(_dot(pr.astype(BF16), v) / jnp.sum(pr, axis=-1, keepdims=True))
        o_ref[:, cols] = jnp.where(first_half, outs[0], outs[1]).astype(o_ref.dtype)


def _softmax_sample_kernel(*refs, fox, past, total, n_heads):
    if fox:
        q_ref, kc_ref, vc_ref, kn_ref, vn_ref, fq_ref, fk_ref, o_ref = refs
    else:
        q_ref, kc_ref, vc_ref, kn_ref, vn_ref, o_ref = refs
    t_rows = o_ref.shape[0]
    qpos = past + lax.broadcasted_iota(jnp.int32, (t_rows, total), 0)
    kpos = lax.broadcasted_iota(jnp.int32, (t_rows, total), 1)
    if fox:
        ok = kpos <= qpos
    else:
        ok = (kpos < past + t_rows) & (kpos // CHUNK <= qpos // CHUNK)
    for hh in range(n_heads):
        cols = slice(hh * LANE, (hh + 1) * LANE)
        if fox:
            q = q_ref[:, cols]
            k = _join_rows(kc_ref[:, cols], kn_ref[:, cols], total)
        else:
            q = q_ref[hh]
            k = _join_rows(kc_ref[hh], kn_ref[hh], total)
        v = _join_rows(vc_ref[:, cols], vn_ref[:, cols], total)
        s = _dot_nt(q, k)
        if fox:
            s = (fq_ref[:, hh:hh + 1] + fk_ref[hh:hh + 1, :]) * LOG2E + s
        s = jnp.where(ok, s, NEG_INF)
        mx = jnp.max(s, axis=-1, keepdims=True)
        pr = jnp.exp2(s - mx)
        o = _dot(pr.astype(BF16), v) / jnp.sum(pr, axis=-1, keepdims=True)
        o_ref[:, cols] = o.astype(o_ref.dtype)


def _sb_sample_kernel(q_ref, kc_ref, vc_ref, kn_ref, vn_ref, o_ref, *, past, total, n_heads):
    t_rows = o_ref.shape[0]
    nb = total // LANE
    qpos = past + lax.broadcasted_iota(jnp.int32, (t_rows, total), 0)
    kpos = lax.broadcasted_iota(jnp.int32, (t_rows, total), 1)
    ok = kpos < qpos
    tri = _suffix_tri(LANE)
    for hh in range(n_heads):
        cols = slice(hh * LANE, (hh + 1) * LANE)
        k = _join_rows(kc_ref[:, cols], kn_ref[:, cols], total)
        v = _join_rows(vc_ref[:, cols], vn_ref[:, cols], total)
        z = _dot_nt(q_ref[:, cols], k)
        log_beta = _log_sigmoid(z)
        log_fail = jnp.where(ok, log_beta - z, 0.0)
        stacked = jnp.concatenate([log_fail[:, c * LANE:(c + 1) * LANE] for c in range(nb)], axis=0)
        hi, lo = _split2(stacked)
        incl = _dot(hi, tri) + _dot(lo, tri)
        across = jnp.zeros((t_rows, 1), F32)
        w_blocks = [None] * nb
        for c in reversed(range(nb)):
            sl = slice(c * LANE, (c + 1) * LANE)
            inc = incl[c * t_rows:(c + 1) * t_rows]
            w = jnp.exp(log_beta[:, sl] + (inc - log_fail[:, sl] + across))
            w_blocks[c] = jnp.where(ok[:, sl], w, 0.0).astype(BF16)
            across = across + inc[:, 0:1]
        o_ref[:, cols] = _dot(jnp.concatenate(w_blocks, axis=-1), v).astype(o_ref.dtype)


def _sample_specs(n_streams, t_rows, past, gw):
    row = lambda b: (b, 0)
    return [pl.BlockSpec((t_rows, gw), row), pl.BlockSpec((past, gw), row), pl.BlockSpec((past, gw), row),
            pl.BlockSpec((t_rows, gw), row), pl.BlockSpec((t_rows, gw), row)]


def _pad_keys(n):
    return -(-n // LANE) * LANE


def _band_sample(q, kc, vc, kn, vn, table, n_streams):
    m, gw = q.shape
    t_rows, past = m // n_streams, kc.shape[0] // n_streams
    total = table.shape[-1]
    return pl.pallas_call(
        functools.partial(_band_sample_kernel, total=total),
        grid=(n_streams,),
        in_specs=_sample_specs(n_streams, t_rows, past, gw) + [_resident(table.shape)],
        out_specs=pl.BlockSpec((t_rows, gw), lambda b: (b, 0)),
        out_shape=jax.ShapeDtypeStruct((m, gw), BF16),
        compiler_params=_cparams("parallel"), name="band_sample",
    )(q, kc, vc, kn, vn, table)


def _fox_sample(q, kc, vc, kn, vn, f_rows, f_cols, n_streams):
    m, gw = q.shape
    t_rows, past = m // n_streams, kc.shape[0] // n_streams
    total = f_cols.shape[-1]
    return pl.pallas_call(
        functools.partial(_softmax_sample_kernel, fox=True, past=past, total=total,
                          n_heads=L_HEADS),
        grid=(n_streams,),
        in_specs=_sample_specs(n_streams, t_rows, past, gw) + [
            pl.BlockSpec((t_rows, LANE), lambda b: (b, 0)),
            pl.BlockSpec((None, L_HEADS, total), lambda b: (b, 0, 0))],
        out_specs=pl.BlockSpec((t_rows, gw), lambda b: (b, 0)),
        out_shape=jax.ShapeDtypeStruct((m, gw), BF16),
        compiler_params=_cparams("parallel"), name="fox_sample",
    )(q, kc, vc, kn, vn, f_rows, f_cols)


def _sb_sample(q, kc, vc, kn, vn, n_streams):
    m, gw = q.shape
    t_rows, past = m // n_streams, kc.shape[0] // n_streams
    total = _pad_keys(past + t_rows)
    return pl.pallas_call(
        functools.partial(_sb_sample_kernel, past=past, total=total, n_heads=L_HEADS),
        grid=(n_streams,),
        in_specs=_sample_specs(n_streams, t_rows, past, gw),
        out_specs=pl.BlockSpec((t_rows, gw), lambda b: (b, 0)),
        out_shape=jax.ShapeDtypeStruct((m, gw), BF16),
        compiler_params=_cparams("parallel"), name="sb_sample",
    )(q, kc, vc, kn, vn)


def _mla_sample(qd, kd_c, vd_c, kd_n, vd_n, n_streams):
    n_heads, m, dk = qd.shape
    gw = vd_n.shape[1]
    t_rows, past = m // n_streams, vd_c.shape[0] // n_streams
    total = _pad_keys(past + t_rows)
    lat = lambda rows: pl.BlockSpec((n_heads, rows, dk), lambda b: (0, b, 0))
    row = lambda b: (b, 0)
    return pl.pallas_call(
        functools.partial(_softmax_sample_kernel, fox=False, past=past,
                          total=total, n_heads=n_heads),
        grid=(n_streams,),
        in_specs=[lat(t_rows), lat(past), pl.BlockSpec((past, gw), row), lat(t_rows),
                  pl.BlockSpec((t_rows, gw), row)],
        out_specs=pl.BlockSpec((t_rows, gw), row),
        out_shape=jax.ShapeDtypeStruct((m, gw), BF16),
        compiler_params=_cparams("parallel"), name="mla_sample",
    )(qd, kd_c, vd_c, kd_n, vd_n)


def _out_proj_kernel(oa_ref, ob_ref, oc_ref, od_ref, gmix_ref, wo_ref, x_ref, gpost_ref, gffn_ref,
                     x1_ref, h2_ref):
    gw = oa_ref.shape[1]
    y = None
    for gi, o_ref in enumerate((oa_ref, ob_ref, oc_ref, od_ref)):
        n = _rms(o_ref[...].astype(F32), gmix_ref[gi:gi + 1, :]).astype(BF16)
        part = _dot(n, wo_ref[gi * gw:(gi + 1) * gw, :])
        y = part if y is None else y + part
    x1 = x_ref[...] + _rms(y, gpost_ref[...])
    x1_ref[...] = x1
    h2_ref[...] = _rms(x1, gffn_ref[...]).astype(BF16)


def _out_proj(outs, g_mix, w_o, x, g_post, g_ffn, tm):
    m, d = x.shape
    gw = d // 4
    rows = lambda i: (i, 0)
    return pl.pallas_call(
        _out_proj_kernel, grid=(m // tm,),
        in_specs=[pl.BlockSpec((tm, gw), rows)] * 4 + [_resident(g_mix.shape), _resident(w_o.shape),
                                                       pl.BlockSpec((tm, d), rows),
                                                       _resident((1, d)), _resident((1, d))],
        out_specs=[pl.BlockSpec((tm, d), rows), pl.BlockSpec((tm, d), rows)],
        out_shape=[jax.ShapeDtypeStruct((m, d), F32), jax.ShapeDtypeStruct((m, d), BF16)],
        compiler_params=_cparams("parallel"), name="out_proj",
    )(*outs, g_mix, w_o, x, g_post, g_ffn)


FFN_ROW_GROUP = 256


def _ffn_kernel(h_ref, wg_ref, wu_ref, wd_ref, x_ref, g_ref, o_ref, acc_ref):
    f = pl.program_id(1)

    @pl.when(f == 0)
    def _():
        acc_ref[...] = jnp.zeros_like(acc_ref)

    tm = h_ref.shape[0]
    rg = min(FFN_ROW_GROUP, tm)

    def gate_up(r):
        h = h_ref[r * rg:(r + 1) * rg, :]
        return _dot(h, wg_ref[...]), _dot(h, wu_ref[...])

    pending = [gate_up(0)]
    for r in range(tm // rg):
        if r + 1 < tm // rg:
            pending.append(gate_up(r + 1))
        gate, up = pending.pop(0)
        act = (gate * jax.nn.sigmoid(gate) * up).astype(BF16)
        acc_ref[r * rg:(r + 1) * rg, :] += _dot(act, wd_ref[...])

    @pl.when(f == pl.num_programs(1) - 1)
    def _():
        o_ref[...] = x_ref[...] + _rms(acc_ref[...], g_ref[...])


def _ffn(h2, w_gate, w_up, w_down, x1, g, tm, tf):
    m, d = x1.shape
    ff = w_gate.shape[1]
    rows = lambda i, f: (i, 0)
    return pl.pallas_call(
        _ffn_kernel, grid=(m // tm, ff // tf),
        in_specs=[pl.BlockSpec((tm, d), rows), pl.BlockSpec((d, tf), lambda i, f: (0, f)),
                  pl.BlockSpec((d, tf), lambda i, f: (0, f)), pl.BlockSpec((tf, d), lambda i, f: (f, 0)),
                  pl.BlockSpec((tm, d), rows), pl.BlockSpec((1, d), lambda i, f: (0, 0))],
        out_specs=pl.BlockSpec((tm, d), rows),
        out_shape=jax.ShapeDtypeStruct((m, d), F32),
        scratch_shapes=[pltpu.VMEM((tm, d), F32)],
        compiler_params=_cparams("parallel", "arbitrary"), name="ffn",
    )(h2, w_gate, w_up, w_down, x1, g)


def _pad_cols(w, n):
    return jnp.pad(w, ((0, 0), (0, n - w.shape[1])))


def _swap_halves(w):
    half = w.shape[1] // 2
    return jnp.concatenate([w[:, half:], w[:, :half]], axis=1)


def _layer_weights(l, d, w_in, b_f, g_q_lat, g_kv_lat, w_uq, w_uk, w_uv):
    gw = d // 4
    q_rank, kv_rank = g_q_lat.shape[1], g_kv_lat.shape[1]
    sizes = (3 * gw, 3 * gw, L_HEADS, 3 * gw, q_rank, kv_rank, ROPE_DIM)
    za, zb, zf, zc, dq, dkv, dkr = jnp.split(w_in[l], np.cumsum(sizes)[:-1].tolist(), axis=1)
    w_all = jnp.concatenate([za, zb, zc, dq, dkv, _pad_cols(dkr, LANE), _pad_cols(_swap_halves(dkr), LANE),
                             _pad_cols(zf, LANE)], axis=1).astype(BF16)
    hd = NOPE_DIM + ROPE_DIM
    uq = w_uq[l]
    nope = [uq[:, h * hd:h * hd + NOPE_DIM] for h in range(L_HEADS)]
    rope = [uq[:, h * hd + NOPE_DIM:(h + 1) * hd] for h in range(L_HEADS)]
    w_q = jnp.concatenate(nope + [_pad_cols(r, LANE) for r in rope], axis=1).astype(BF16)
    w_qs = jnp.concatenate([_pad_cols(_swap_halves(r), LANE) for r in rope], axis=1).astype(BF16)
    return dict(w_in=w_all, b_f=_pad_cols(b_f[l][None, :].astype(F32), LANE), g_q=g_q_lat[l][None, :],
                g_kv=g_kv_lat[l][None, :], w_q=w_q, w_qs=w_qs,
                w_uk=w_uk[l].astype(BF16), w_uv=w_uv[l].astype(BF16))


def _rope_tables(pos):
    half = ROPE_DIM // 2
    inv_freq = ROPE_THETA ** (-jnp.arange(half, dtype=F32) / half)
    ang = pos.astype(F32)[:, None] * inv_freq[None, :]
    cos, sin = jnp.cos(ang), jnp.sin(ang)
    zero = jnp.zeros((pos.shape[0], LANE - ROPE_DIM), F32)
    return jnp.concatenate([cos, cos, zero], axis=1), jnp.concatenate([-sin, sin, zero], axis=1)


def kernel(x_prompt, x_sample, cache_a_k, cache_a_v, cache_b_k, cache_b_v, cache_b_logf, cache_c_k, cache_c_v,
           cache_d_ckv, cache_d_kpe, w_in, b_f, rel_bias, g_q_lat, g_kv_lat, w_uq, w_uk, w_uv, g_mix, w_o,
           g_pre_attn, g_post_attn, g_pre_ffn, g_post_ffn, w_gate, w_up, w_down):
    batch, seq, d = x_prompt.shape
    n_streams, t_rows, _ = x_sample.shape
    depth = w_in.shape[0]
    past = cache_b_k.shape[2]
    a_win = cache_a_k.shape[2]
    gw = d // 4
    mp, ms = batch * seq, n_streams * t_rows
    bq, rb = 512, 128
    tm_p, tm_s = 256, ms

    cos_p, sin_p = _rope_tables(jnp.arange(seq))
    cos_s, sin_s = _rope_tables(jnp.tile(past + jnp.arange(t_rows), n_streams))

    xp = x_prompt.reshape(mp, d)
    xs = x_sample.reshape(ms, d)
    p_states, s_states = [], []
    stacked = None
    for l in range(depth):
        lw = _layer_weights(l, d, w_in, b_f, g_q_lat, g_kv_lat, w_uq, w_uk, w_uv)
        g_attn = g_pre_attn[l][None, :]
        wo = w_o[l].astype(BF16)
        wg, wu, wd = w_gate[l].astype(BF16), w_up[l].astype(BF16), w_down[l].astype(BF16)
        g_post, g_ffn, g_out = g_post_attn[l][None, :], g_pre_ffn[l][None, :], g_post_ffn[l][None, :]

        pr = _project(xp, g_attn, lw, cos_p, sin_p, tm_p, stack=(l, depth, stacked))
        stacked = {name: pr[name] for name in _STACKED_STATES}
        f_run = _cumsum(pr['logf'], seq)
        f_cols = (-f_run[:, :L_HEADS]).reshape(batch, seq, L_HEADS).transpose(0, 2, 1)
        f_cols = f_cols.reshape(batch, L_HEADS, seq // bq, bq)
        bq_flash = 2 * bq if seq % (2 * bq) == 0 else bq
        kd, vd = _mla_keys(pr['ckv'], pr['kpe128'], lw['w_uk'], lw['w_uv'], 512)
        oa = _band_prompt(pr['qa'], pr['ka16'], pr['va16'], _band_table(rel_bias[l], rb), batch, seq, bq, rb)
        ob = _fox_prompt(pr['qb'], pr['kb16'], pr['vb16'], f_run, f_cols, batch, seq, bq_flash, bq, 2 * rb)
        oc = _sb_prompt(pr['qc'], pr['kc16'], pr['vc16'], batch, seq, bq, rb)
        od = _mla_prompt(pr['qd'], kd, vd, batch, seq, bq_flash, bq, 2 * rb)
        x1, h2 = _out_proj((oa, ob, oc, od), g_mix[l], wo, xp, g_post, g_ffn, tm_p)
        xp = _ffn(h2, wg, wu, wd, x1, g_out, 512, 512)
        keep = min(a_win, seq)
        p_states.append((
            pr['ka'].reshape(batch, seq, A_HEADS, gw // A_HEADS)[:, seq - keep:],
            pr['va'].reshape(batch, seq, A_HEADS, gw // A_HEADS)[:, seq - keep:],
            pr['logf'][:, :L_HEADS].reshape(batch, seq, L_HEADS),
            pr['ckv'].reshape(batch, seq, -1), pr['kpe'].reshape(batch, seq, ROPE_DIM)))

        sr = _project(xs, g_attn, lw, cos_s, sin_s, tm_s)
        qa, ka, va, qb, kb, vb, qc, kc, vc = (sr[n] for n in ('qa', 'ka', 'va', 'qb', 'kb', 'vb', 'qc', 'kc', 'vc'))
        logf, qd, ckv, kpe, kpe128 = (sr[n] for n in ('logf', 'qd', 'ckv', 'kpe', 'kpe128'))
        tot_a = _pad_keys(a_win + t_rows)
        rel = (a_win + np.arange(t_rows))[:, None] - np.arange(tot_a)[None, :]
        tab = rel_bias[l][:, np.clip(rel, -REL_CLIP, REL_CLIP) + REL_CLIP].astype(F32)
        tab = jnp.where(jnp.asarray(np.arange(tot_a) < a_win + t_rows)[None, None, :], tab, NEG_INF)
        oa = _band_sample(qa, cache_a_k[l].reshape(n_streams * a_win, gw),
                          cache_a_v[l].reshape(n_streams * a_win, gw), ka, va, tab, n_streams)
        f_new = _cumsum(logf, t_rows)
        cl = _pad_cols(cache_b_logf[l].reshape(n_streams * past, L_HEADS).astype(F32), LANE)
        suffix = _cumsum(cl, past, reverse=True)
        tot = _pad_keys(past + t_rows)
        f_cols = jnp.concatenate([suffix[:, :L_HEADS].reshape(n_streams, past, L_HEADS),
                                  -f_new[:, :L_HEADS].reshape(n_streams, t_rows, L_HEADS),
                                  jnp.zeros((n_streams, tot - past - t_rows, L_HEADS), F32)], axis=1)
        f_cols = f_cols.transpose(0, 2, 1)
        ob = _fox_sample(qb, cache_b_k[l].reshape(n_streams * past, gw),
                         cache_b_v[l].reshape(n_streams * past, gw), kb, vb, f_new, f_cols, n_streams)
        oc = _sb_sample(qc, cache_c_k[l].reshape(n_streams * past, gw),
                        cache_c_v[l].reshape(n_streams * past, gw), kc, vc, n_streams)
        kd_n, vd_n = _mla_keys(ckv, kpe128, lw['w_uk'], lw['w_uv'], ms)
        kd_c, vd_c = _mla_keys(cache_d_ckv[l].reshape(n_streams * past, -1),
                               _pad_cols(cache_d_kpe[l].reshape(n_streams * past, ROPE_DIM), LANE),
                               lw['w_uk'], lw['w_uv'], 512)
        od = _mla_sample(qd, kd_c, vd_c, kd_n, vd_n, n_streams)
        x1, h2 = _out_proj((oa, ob, oc, od), g_mix[l], wo, xs, g_post, g_ffn, tm_s)
        xs = _ffn(h2, wg, wu, wd, x1, g_out, ms, 512)
        s_states.append((
            ka.reshape(n_streams, t_rows, A_HEADS, gw // A_HEADS),
            va.reshape(n_streams, t_rows, A_HEADS, gw // A_HEADS),
            kb.reshape(n_streams, t_rows, L_HEADS, LANE), vb.reshape(n_streams, t_rows, L_HEADS, LANE),
            logf[:, :L_HEADS].reshape(n_streams, t_rows, L_HEADS),
            kc.reshape(n_streams, t_rows, L_HEADS, LANE), vc.reshape(n_streams, t_rows, L_HEADS, LANE),
            ckv.reshape(n_streams, t_rows, -1), kpe.reshape(n_streams, t_rows, ROPE_DIM)))

    p_a_k, p_a_v, p_b_logf, p_d_ckv, p_d_kpe = [jnp.stack(s) for s in zip(*p_states)]
    p_b_k, p_b_v, p_c_k, p_c_v = (stacked[name].reshape(depth, batch, seq, L_HEADS, LANE)
                                  for name in _STACKED_STATES)
    s_out = [jnp.stack(s) for s in zip(*s_states)]
    return (xp.reshape(batch, seq, d), xs.reshape(n_streams, t_rows, d),
            p_a_k, p_a_v, p_b_k, p_b_v, p_b_logf, p_c_k, p_c_v, p_d_ckv, p_d_kpe, *s_out)
```

```python
import functools
import math

import numpy as np
import jax
import jax.numpy as jnp
from jax import lax
from jax.experimental import pallas as pl
from jax.experimental.pallas import tpu as pltpu

CHUNK = 64
A_HEADS = 8
L_HEADS = 4
BAND_CHUNKS = 8
REL_CLIP = 128
NOPE_DIM = 128
ROPE_DIM = 64
ROPE_THETA = 10000.0
RMS_EPS = 1e-6
NEG_INF = -1e30
LOG2E = math.log2(math.e)

LANE = 128
MXU_DEPTH = 256
VMEM_LIMIT = 58 * 1024 * 1024

F32 = jnp.float32
BF16 = jnp.bfloat16


def _cparams(*sem):
    return pltpu.CompilerParams(dimension_semantics=sem, vmem_limit_bytes=VMEM_LIMIT)


def _resident(shape):
    return pl.BlockSpec(shape, lambda *_: (0,) * len(shape), pipeline_mode=pl.Buffered(1))


def _rms(x, g):
    return x * lax.rsqrt(jnp.mean(x * x, axis=-1, keepdims=True) + RMS_EPS) * g


def _log_sigmoid(x):
    return jnp.minimum(x, 0.0) - jnp.log1p(jnp.exp(-jnp.abs(x)))


def _dot(a, b):
    return jnp.dot(a, b, preferred_element_type=F32)


def _dot_nt(a, b):
    return lax.dot_general(a, b, (((1,), (1,)), ((), ())), preferred_element_type=F32)


def _split2(x):
    hi = x.astype(BF16)
    lo = (x - hi.astype(F32)).astype(BF16)
    return hi, lo


def _split3(x):
    hi = x.astype(BF16)
    r = x - hi.astype(F32)
    mid = r.astype(BF16)
    lo = (r - mid.astype(F32)).astype(BF16)
    return hi, mid, lo


_PROJ_INPUTS = 10


def _proj_kernel(*refs, gw, q_rank, kv_rank, rope_dim, q_scales):
    x_ref, g_ref, w_ref, bf_ref, gq_ref, gkv_ref, wq_ref, wqs_ref, cos_ref, sin_ref = refs[:_PROJ_INPUTS]
    (qa_ref, ka_ref, va_ref, ka16_ref, va16_ref, qb_ref, kb_ref, vb_ref, kb16_ref, vb16_ref,
     qc_ref, kc_ref, vc_ref, kc16_ref, vc16_ref, logf_ref, qd_ref, ckv_ref, kpe_ref, kpe128_ref,
     ) = refs[len(refs) - len(_PROJ_OUTPUTS):]
    h = _rms(x_ref[...], g_ref[...]).astype(BF16)

    def seg(a, n):
        return _dot(h, w_ref[:, a:a + n])

    groups = ((qa_ref, ka_ref, va_ref, ka16_ref, va16_ref), (qb_ref, kb_ref, vb_ref, kb16_ref, vb16_ref),
              (qc_ref, kc_ref, vc_ref, kc16_ref, vc16_ref))
    for gi, (q_r, k_r, v_r, k16_r, v16_r) in enumerate(groups):
        base = 3 * gw * gi
        q_r[...] = (seg(base, gw) * q_scales[gi]).astype(BF16)
        k = seg(base + gw, gw)
        k16_r[...] = k.astype(BF16)
        v = seg(base + 2 * gw, gw)
        v16_r[...] = v.astype(BF16)
        for state_ref, val in ((k_r, k), (v_r, v)):
            if state_ref.shape[-1] == gw:
                state_ref[...] = val
            else:
                n_h = gw // LANE
                for hh in range(n_h):
                    state_ref[pl.ds(hh, val.shape[0], stride=n_h), :] = val[:, hh * LANE:(hh + 1) * LANE]
    off = 9 * gw
    dq = seg(off, q_rank)
    off += q_rank
    dkv = seg(off, kv_rank)
    off += kv_rank
    dkr = seg(off, LANE)
    dkr_sw = seg(off + LANE, LANE)
    zf = seg(off + 2 * LANE, LANE)

    logf_ref[...] = _log_sigmoid(zf + bf_ref[...])
    ckv_ref[...] = _rms(dkv, gkv_ref[...])
    cos = cos_ref[...]
    sin = sin_ref[...]
    kpe = dkr * cos + dkr_sw * sin
    kpe128_ref[...] = kpe
    kpe_ref[...] = kpe[:, :rope_dim]

    nq = _rms(dq, gq_ref[...]).astype(BF16)
    qf = _dot(nq, wq_ref[...])
    qs = _dot(nq, wqs_ref[...])
    n_heads = qd_ref.shape[0]
    for hh in range(n_heads):
        qd_ref[hh, :, 0:LANE] = (qf[:, LANE * hh:LANE * (hh + 1)] * q_scales[3]).astype(BF16)
        r0 = LANE * (n_heads + hh)
        rot = qf[:, r0:r0 + LANE] * cos + qs[:, LANE * hh:LANE * (hh + 1)] * sin
        qd_ref[hh, :, LANE:2 * LANE] = (rot * q_scales[3]).astype(BF16)


def _q_scales(gw):
    return ((gw // A_HEADS) ** -0.5 * LOG2E, (gw // L_HEADS) ** -0.5 * LOG2E, (gw // L_HEADS) ** -0.5,
            (NOPE_DIM + ROPE_DIM) ** -0.5 * LOG2E)


_PROJ_OUTPUTS = ('qa', 'ka', 'va', 'ka16', 'va16', 'qb', 'kb', 'vb', 'kb16', 'vb16',
                 'qc', 'kc', 'vc', 'kc16', 'vc16', 'logf', 'qd', 'ckv', 'kpe', 'kpe128')


_STACKED_STATES = ('kb', 'vb', 'kc', 'vc')


def _project(x, g, lw, cos_t, sin_t, tm, stack=None):
    m, d = x.shape
    gw = d // 4
    npad = lw['w_in'].shape[1]
    q_rank, kv_rank = lw['g_q'].shape[1], lw['g_kv'].shape[1]
    n_pos_blocks = cos_t.shape[0] // tm
    rows = lambda i: (i, 0)
    pos = lambda i: (i % n_pos_blocks, 0)
    f32o = lambda n: jax.ShapeDtypeStruct((m, n), F32)
    bf16o = lambda n: jax.ShapeDtypeStruct((m, n), BF16)
    blk = lambda n: pl.BlockSpec((tm, n), rows)
    group = [bf16o(gw), f32o(gw), f32o(gw), bf16o(gw), bf16o(gw)]
    out_shape = group * 3 + [f32o(LANE), jax.ShapeDtypeStruct((L_HEADS, m, 2 * LANE), BF16),
                             f32o(kv_rank), f32o(ROPE_DIM), f32o(LANE)]
    out_specs = [blk(gw)] * 15 + [blk(LANE), pl.BlockSpec((L_HEADS, tm, 2 * LANE), lambda i: (0, i, 0)),
                                  blk(kv_rank), blk(ROPE_DIM), blk(LANE)]
    in_specs = [blk(d), _resident((1, d)), _resident((d, npad)), _resident((1, LANE)),
                _resident((1, q_rank)), _resident((1, kv_rank)),
                _resident(lw['w_q'].shape), _resident(lw['w_qs'].shape),
                pl.BlockSpec((tm, LANE), pos), pl.BlockSpec((tm, LANE), pos)]
    args = [x, g, lw['w_in'], lw['b_f'], lw['g_q'], lw['g_kv'], lw['w_q'], lw['w_qs'], cos_t, sin_t]
    aliases = {}
    if stack is not None:
        layer, depth, buffers = stack
        for name in _STACKED_STATES:
            o = _PROJ_OUTPUTS.index(name)
            n_h = gw // LANE
            out_shape[o] = jax.ShapeDtypeStruct((depth, m * n_h, LANE), F32)
            out_specs[o] = pl.BlockSpec((None, tm * n_h, LANE), lambda i: (layer, i, 0))
            if buffers is not None:
                aliases[len(args)] = o
                args.append(buffers[name])
                in_specs.append(pl.BlockSpec(memory_space=pl.ANY))
    kern = functools.partial(_proj_kernel, gw=gw, q_rank=q_rank, kv_rank=kv_rank, rope_dim=ROPE_DIM,
                             q_scales=_q_scales(gw))
    outs = pl.pallas_call(
        kern, grid=(m // tm,), in_specs=in_specs, out_specs=out_specs, out_shape=out_shape,
        input_output_aliases=aliases, compiler_params=_cparams("parallel"), name="in_proj",
    )(*args)
    return dict(zip(_PROJ_OUTPUTS, outs))


def _mla_keys_kernel(ckv_ref, kpe_ref, wuk_ref, wuv_ref, kd_ref, vd_ref):
    c = ckv_ref[...].astype(BF16)
    kn = _dot(c, wuk_ref[...])
    vd_ref[...] = _dot(c, wuv_ref[...]).astype(BF16)
    kp = kpe_ref[...].astype(BF16)
    for hh in range(kd_ref.shape[0]):
        kd_ref[hh, :, 0:LANE] = kn[:, LANE * hh:LANE * (hh + 1)].astype(BF16)
        kd_ref[hh, :, LANE:2 * LANE] = kp


def _mla_keys(ckv, kpe128, w_uk, w_uv, tm, m=None, first_block=0):
    r = ckv.shape[1]
    m = ckv.shape[0] if m is None else m
    n = w_uk.shape[1]
    rows = lambda i: (i, 0)
    src = lambda i: (first_block + i, 0)
    return pl.pallas_call(
        _mla_keys_kernel, grid=(m // tm,),
        in_specs=[pl.BlockSpec((tm, r), src), pl.BlockSpec((tm, LANE), src),
                  _resident(w_uk.shape), _resident(w_uv.shape)],
        out_specs=[pl.BlockSpec((L_HEADS, tm, 2 * LANE), lambda i: (0, i, 0)), pl.BlockSpec((tm, n), rows)],
        out_shape=[jax.ShapeDtypeStruct((L_HEADS, m, 2 * LANE), BF16), jax.ShapeDtypeStruct((m, n), BF16)],
        compiler_params=_cparams("parallel"), name="mla_keys",
    )(ckv, kpe128, w_uk, w_uv)


def _cumsum_kernel(x_ref, o_ref, carry_ref, *, seg, reverse):
    @pl.when(pl.program_id(1) == 0)
    def _():
        carry_ref[...] = jnp.zeros_like(carry_ref)

    x = x_ref[...]
    tc = x.shape[0]
    r = lax.broadcasted_iota(jnp.int32, (tc, tc), 0)
    c = lax.broadcasted_iota(jnp.int32, (tc, tc), 1)
    keep = (c > r) if reverse else (c <= r)
    if seg < tc:
        keep = keep & (r // seg == c // seg)
    tri = jnp.where(keep, 1.0, 0.0).astype(BF16)
    hi, mid, lo = _split3(x)
    y = _dot(tri, hi) + _dot(tri, mid) + _dot(tri, lo)
    o_ref[...] = y + carry_ref[...]
    if seg > tc:
        carry_ref[...] += jnp.sum(x, axis=0, keepdims=True)


def _cumsum(x, seg, reverse=False):
    m, n = x.shape
    tc = min(512, m) if seg >= 512 else min(256, m)
    nb = max(seg // tc, 1)
    nseg = m // (nb * tc)
    if reverse:
        idx = lambda s, j: (s * nb + nb - 1 - j, 0)
    else:
        idx = lambda s, j: (s * nb + j, 0)
    return pl.pallas_call(
        functools.partial(_cumsum_kernel, seg=seg, reverse=reverse), grid=(nseg, nb),
        in_specs=[pl.BlockSpec((tc, n), idx)], out_specs=pl.BlockSpec((tc, n), idx),
        out_shape=jax.ShapeDtypeStruct((m, n), F32),
        scratch_shapes=[pltpu.VMEM((1, n), F32)],
        compiler_params=_cparams("parallel", "arbitrary"), name="cumsum",
    )(x)


def _lane_tile(x, n):
    return x if n == LANE else jnp.concatenate([x] * (n // LANE), axis=1)


def _flash_kernel(*refs, fox, bq, bk, rb, ahead):
    if fox:
        q_ref, k_ref, v_ref, fq_ref, fk_ref, o_ref, acc_ref, m_ref, l_ref = refs
    else:
        q_ref, k_ref, v_ref, o_ref, acc_ref, m_ref, l_ref = refs
    head = pl.program_id(1)
    qi = pl.program_id(2)
    m_ref[...] = jnp.full_like(m_ref, NEG_INF)
    l_ref[...] = jnp.zeros_like(l_ref)
    acc_ref[...] = jnp.zeros_like(acc_ref)
    if fox:
        lane = lax.broadcasted_iota(jnp.int32, fq_ref.shape, 1)
        fq = jnp.sum(jnp.where(lane == head, fq_ref[...], 0.0), axis=-1, keepdims=True) * LOG2E

    def logits(r, j):
        rows = slice(r * rb, (r + 1) * rb)
        s = _dot_nt(q_ref[rows, :], k_ref[pl.ds(pl.multiple_of(j * bk, bk), bk), :])
        if fox:
            s = fq[rows] + (fk_ref[pl.ds(j, 1), :] * LOG2E + s)
        return s

    def update(r, s, j):
        rows = slice(r * rb, (r + 1) * rb)
        m_prev = m_ref[rows, :]
        m_new = jnp.maximum(m_prev, jnp.max(s, axis=-1, keepdims=True))
        alpha = jnp.exp2(m_prev - m_new)
        p = jnp.exp2(s - _lane_tile(m_new, s.shape[1]))
        l_ref[rows, :] = alpha * l_ref[rows, :] + jnp.sum(p, axis=-1, keepdims=True)
        v = v_ref[pl.ds(pl.multiple_of(j * bk, bk), bk), :]
        acc_ref[rows, :] = alpha * acc_ref[rows, :] + _dot(p.astype(BF16), v)
        m_ref[rows, :] = m_new

    def run(items):
        pending = [logits(r, j) for r, j, _ in items[:ahead]]
        for i, (r, j, diag) in enumerate(items):
            if i + ahead < len(items):
                pending.append(logits(*items[i + ahead][:2]))
            s = pending.pop(0)
            if diag is not None:
                row = r * rb + lax.broadcasted_iota(jnp.int32, s.shape, 0)
                col = diag + lax.broadcasted_iota(jnp.int32, s.shape, 1)
                ok = (col <= row) if fox else (col // CHUNK <= row // CHUNK)
                s = jnp.where(ok, s, NEG_INF)
            update(r, s, j)

    n_groups = bq // rb
    kb_per_q = bq // bk

    def full_block(j, carry):
        run([(r, j, None) for r in range(n_groups)])
        return carry

    lax.fori_loop(0, qi * kb_per_q, full_block, 0)

    items = []
    for kb in range(kb_per_q):
        first = kb * bk // rb
        items += [(r, qi * kb_per_q + kb, kb * bk) for r in range(first, first + bk // rb)]
        items += [(r, qi * kb_per_q + kb, None) for r in range(first + bk // rb, n_groups)]
    run(items)
    o_ref[...] = (acc_ref[...] / l_ref[...]).astype(o_ref.dtype)


def _flash_scratch(bq):
    return [pltpu.VMEM((bq, LANE), F32), pltpu.VMEM((bq, LANE), F32), pltpu.VMEM((bq, LANE), F32)]


def _fox_prompt(q, k, v, f_rows, f_cols, batch, seq, bq, bk, rb, ahead):
    m, gw = q.shape
    nq = seq // bq
    qmap = lambda b, h, i: (b * nq + i, h)
    kvmap = lambda b, h, i: (b, h)
    return pl.pallas_call(
        functools.partial(_flash_kernel, fox=True, bq=bq, bk=bk, rb=rb, ahead=ahead),
        grid=(batch, L_HEADS, nq),
        in_specs=[pl.BlockSpec((bq, LANE), qmap), pl.BlockSpec((seq, LANE), kvmap),
                  pl.BlockSpec((seq, LANE), kvmap),
                  pl.BlockSpec((bq, LANE), lambda b, h, i: (b * nq + i, 0)),
                  pl.BlockSpec((None, None, seq // bk, bk), lambda b, h, i: (b, h, 0, 0))],
        out_specs=pl.BlockSpec((bq, LANE), qmap),
        out_shape=jax.ShapeDtypeStruct((m, gw), BF16),
        scratch_shapes=_flash_scratch(bq),
        compiler_params=_cparams("parallel", "parallel", "arbitrary"), name="fox_prompt",
    )(q, k, v, f_rows, f_cols)


def _mla_prompt(qd, kd, vd, batch, seq, bq, bk, rb, ahead):
    n_heads, m, dk = qd.shape
    nq = seq // bq
    return pl.pallas_call(
        functools.partial(_flash_kernel, fox=False, bq=bq, bk=bk, rb=rb, ahead=ahead),
        grid=(batch, n_heads, nq),
        in_specs=[pl.BlockSpec((None, bq, dk), lambda b, h, i: (h, b * nq + i, 0)),
                  pl.BlockSpec((None, seq, dk), lambda b, h, i: (h, b, 0)),
                  pl.BlockSpec((seq, LANE), lambda b, h, i: (b, h))],
        out_specs=pl.BlockSpec((bq, LANE), lambda b, h, i: (b * nq + i, h)),
        out_shape=jax.ShapeDtypeStruct((m, n_heads * LANE), BF16),
        scratch_shapes=_flash_scratch(bq),
        compiler_params=_cparams("parallel", "parallel", "arbitrary"), name="mla_prompt",
    )(qd, kd, vd)


SB_UNDERFLOW = 105.0


def _suffix_tri(n):
    r = lax.broadcasted_iota(jnp.int32, (n, n), 0)
    c = lax.broadcasted_iota(jnp.int32, (n, n), 1)
    return jnp.where(r >= c, 1.0, 0.0).astype(BF16)


def _sb_kernel(q_ref, k_ref, v_ref, o_ref, acc_ref, across_ref, *, bq, rb, sub):
    qi = pl.program_id(2)
    acc_ref[...] = jnp.zeros_like(acc_ref)
    across_ref[...] = jnp.zeros_like(across_ref)
    tri = _suffix_tri(sub)

    n_groups = bq // rb

    def visit(start, diagonal):
        n_keys = [-(-((r + 1) * rb) // sub) * sub if diagonal else bq for r in range(n_groups)]

        def stage_logits(r):
            rows = slice(r * rb, (r + 1) * rb)
            return _dot_nt(q_ref[rows, :], k_ref[pl.ds(start, n_keys[r]), :])

        def stage_sums(r, z):
            log_fail = -jnp.maximum(z, 0.0) - jnp.log(1.0 + jnp.exp(-jnp.abs(z)))
            log_beta = log_fail + z
            ok = None
            if diagonal:
                row = r * rb + lax.broadcasted_iota(jnp.int32, z.shape, 0)
                col = lax.broadcasted_iota(jnp.int32, z.shape, 1)
                ok = col < row
                log_fail = jnp.where(ok, log_fail, 0.0)
            incl = {}
            for c in reversed(range(n_keys[r] // sub)):
                hi, lo = _split2(log_fail[:, c * sub:(c + 1) * sub])
                incl[c] = _dot(hi, tri) + _dot(lo, tri)
            return log_fail, log_beta, ok, incl

        def stage_values(r, state):
            log_fail, log_beta, ok, incl = state
            rows = slice(r * rb, (r + 1) * rb)
            across = across_ref[rows, :]
            acc = acc_ref[rows, :]
            for c in reversed(range(n_keys[r] // sub)):
                sl = slice(c * sub, (c + 1) * sub)
                w = jnp.exp(log_beta[:, sl] + (incl[c] - log_fail[:, sl] + _lane_tile(across, sub)))
                if diagonal:
                    w = jnp.where(ok[:, sl], w, 0.0)
                vb = v_ref[pl.ds(pl.multiple_of(start + c * sub, sub), sub), :]
                acc = acc + _dot(w.astype(BF16), vb)
                across = across + jnp.broadcast_to(incl[c][:, 0:1], across.shape)
            across_ref[rows, :] = across
            acc_ref[rows, :] = acc

        zs = [stage_logits(r) for r in range(n_groups)]
        states = [stage_sums(r, zs[r]) for r in range(min(2, n_groups))]
        for r in range(n_groups):
            stage_values(r, states[r])
            if r + 2 < n_groups:
                states.append(stage_sums(r + 2, zs[r + 2]))

    visit(pl.multiple_of(qi * bq, bq), True)

    def more(carry):
        t, worst = carry
        return jnp.logical_and(t < qi, worst > -SB_UNDERFLOW)

    def earlier_block(carry):
        t, _ = carry
        visit(pl.multiple_of((qi - 1 - t) * bq, bq), False)
        return t + 1, jnp.max(across_ref[...])

    lax.while_loop(more, earlier_block, (jnp.int32(0), jnp.max(across_ref[...])))
    o_ref[...] = acc_ref[...].astype(o_ref.dtype)


def _sb_prompt(q, k, v, batch, seq, bq, rb):
    m, gw = q.shape
    nq = seq // bq
    qmap = lambda b, h, i: (b * nq + i, h)
    kvmap = lambda b, h, i: (b, h)
    return pl.pallas_call(
        functools.partial(_sb_kernel, bq=bq, rb=rb, sub=MXU_DEPTH),
        grid=(batch, L_HEADS, nq),
        in_specs=[pl.BlockSpec((bq, LANE), qmap), pl.BlockSpec((seq, LANE), kvmap),
                  pl.BlockSpec((seq, LANE), kvmap)],
        out_specs=pl.BlockSpec((bq, LANE), qmap),
        out_shape=jax.ShapeDtypeStruct((m, gw), BF16),
        scratch_shapes=[pltpu.VMEM((bq, LANE), F32), pltpu.VMEM((bq, LANE), F32)],
        compiler_params=_cparams("parallel", "parallel", "arbitrary"), name="sb_prompt",
    )(q, k, v)


BAND_AHEAD = 2


def _band_kernel(q_ref, kp_ref, kc_ref, vp_ref, vc_ref, t_ref, o_ref, *, bq, rb):
    band = BAND_CHUNKS * CHUNK
    has_prev = pl.program_id(2) > 0
    first_half = lax.broadcasted_iota(jnp.int32, (rb, LANE), 1) < (LANE // 2)

    def band_cols(r):
        lo = bq - band + r * rb
        n_cur = (r + 1) * rb
        return lo, n_cur

    def stage_logits(r, hh):
        rows = slice(r * rb, (r + 1) * rb)
        lo, n_cur = band_cols(r)
        q = q_ref[rows, :]
        sel = first_half if hh == 0 else jnp.logical_not(first_half)
        qh = jnp.where(sel, q, jnp.zeros_like(q))
        return _dot_nt(qh, kc_ref[0:n_cur, :]), _dot_nt(qh, kp_ref[lo:bq, :])

    def stage_softmax(r, hh, raw):
        lo, n_cur = band_cols(r)
        n_prev = bq - lo
        sc = t_ref[hh, :, n_prev:n_prev + n_cur] + raw[0]
        sp = t_ref[hh, :, 0:n_prev] + raw[1]
        sp = jnp.where(has_prev, sp, NEG_INF)
        mx = jnp.maximum(jnp.max(sc, axis=-1, keepdims=True), jnp.max(sp, axis=-1, keepdims=True))
        pc = jnp.exp2(sc - mx)
        pp = jnp.exp2(sp - mx)
        den = jnp.sum(pc, axis=-1, keepdims=True) + jnp.sum(pp, axis=-1, keepdims=True)
        return (_dot(pc.astype(BF16), vc_ref[0:n_cur, :]) + _dot(pp.astype(BF16), vp_ref[lo:bq, :])) / den

    items = [(r, hh) for r in range(bq // rb) for hh in range(2)]
    pending = [stage_logits(*it) for it in items[:BAND_AHEAD]]
    out_first = None
    for i, (r, hh) in enumerate(items):
        if i + BAND_AHEAD < len(items):
            pending.append(stage_logits(*items[i + BAND_AHEAD]))
        out = stage_softmax(r, hh, pending.pop(0))
        if hh == 0:
            out_first = out
        else:
            o_ref[r * rb:(r + 1) * rb, :] = jnp.where(first_half, out_first, out).astype(o_ref.dtype)


def _band_table_kernel(g_ref, o_ref):
    rb, width = o_ref.shape
    x = jnp.broadcast_to(g_ref[...], (rb, g_ref.shape[-1]))
    y = pltpu.roll(x, 0, 1, stride=1, stride_axis=0)[:, rb:]
    i = lax.broadcasted_iota(jnp.int32, (rb, width), 0)
    j = lax.broadcasted_iota(jnp.int32, (rb, width), 1)
    first = (i // CHUNK) * CHUNK
    ok = (j >= first) & (j < first + (BAND_CHUNKS + 1) * CHUNK)
    o_ref[...] = jnp.where(ok, y * LOG2E, NEG_INF)


def _band_table(rel_bias, rb):
    band = BAND_CHUNKS * CHUNK
    width = band + rb
    rel = band + rb - np.arange(rb + width)
    g = rel_bias[:, np.clip(rel, -REL_CLIP, REL_CLIP) + REL_CLIP].astype(F32)[:, None, :]
    n_heads = rel_bias.shape[0]
    return pl.pallas_call(
        _band_table_kernel, grid=(n_heads,),
        in_specs=[pl.BlockSpec((None, 1, rb + width), lambda h: (h, 0, 0))],
        out_specs=pl.BlockSpec((None, rb, width), lambda h: (h, 0, 0)),
        out_shape=jax.ShapeDtypeStruct((n_heads, rb, width), F32),
        compiler_params=_cparams("parallel"), name="band_table",
    )(g)


def _band_prompt(q, k, v, table, batch, seq, bq, rb):
    assert bq >= BAND_CHUNKS * CHUNK and rb % CHUNK == 0
    m, gw = q.shape
    nq = seq // bq
    cur = lambda p, b, i: (b * nq + i, p)
    prev = lambda p, b, i: (b * nq + jnp.maximum(i - 1, 0), p)
    blk = lambda f: pl.BlockSpec((bq, LANE), f)
    return pl.pallas_call(
        functools.partial(_band_kernel, bq=bq, rb=rb),
        grid=(A_HEADS // 2, batch, nq),
        in_specs=[blk(cur), blk(prev), blk(cur), blk(prev), blk(cur),
                  pl.BlockSpec((2,) + table.shape[1:], lambda p, b, i: (p, 0, 0))],
        out_specs=blk(cur),
        out_shape=jax.ShapeDtypeStruct((m, gw), BF16),
        compiler_params=_cparams("parallel", "parallel", "parallel"), name="band_prompt",
    )(q, k, k, v, v, table)


def _join_rows(cache, new, total):
    pad = total - cache.shape[0] - new.shape[0]
    parts = [cache.astype(BF16), new.astype(BF16)]
    if pad:
        parts.append(jnp.zeros((pad, cache.shape[1]), BF16))
    return jnp.concatenate(parts, axis=0)


def _band_sample_kernel(q_ref, kc_ref, vc_ref, kn_ref, vn_ref, t_ref, o_ref, *, total):
    t_rows = q_ref.shape[0]
    first_half = lax.broadcasted_iota(jnp.int32, (t_rows, LANE), 1) < (LANE // 2)
    for p in range(A_HEADS // 2):
        cols = slice(p * LANE, (p + 1) * LANE)
        q = q_ref[:, cols]
        k = _join_rows(kc_ref[:, cols], kn_ref[:, cols], total)
        v = _join_rows(vc_ref[:, cols], vn_ref[:, cols], total)
        outs = []
        for hh in range(2):
            sel = first_half if hh == 0 else jnp.logical_not(first_half)
            qh = jnp.where(sel, q, jnp.zeros_like(q))
            s = t_ref[2 * p + hh] * LOG2E + _dot_nt(qh, k)
            mx = jnp.max(s, axis=-1, keepdims=True)
            pr = jnp.exp2(s - mx)
            outs.append(_dot(pr.astype(BF16), v) / jnp.sum(pr, axis=-1, keepdims=True))
        o_ref[:, cols] = jnp.where(first_half, outs[0], outs[1]).astype(o_ref.dtype)


def _mla_sample_kernel(q_ref, kc_ref, vc_ref, kn_ref, vn_ref, o_ref, *, past, total, n_heads):
    t_rows = o_ref.shape[0]
    qpos = past + lax.broadcasted_iota(jnp.int32, (t_rows, total), 0)
    kpos = lax.broadcasted_iota(jnp.int32, (t_rows, total), 1)
    ok = (kpos < past + t_rows) & (kpos // CHUNK <= qpos // CHUNK)
    for hh in range(n_heads):
        cols = slice(hh * LANE, (hh + 1) * LANE)
        k = _join_rows(kc_ref[hh], kn_ref[hh], total)
        v = _join_rows(vc_ref[:, cols], vn_ref[:, cols], total)
        s = jnp.where(ok, _dot_nt(q_ref[hh], k), NEG_INF)
        mx = jnp.max(s, axis=-1, keepdims=True)
        pr = jnp.exp2(s - mx)
        o = _dot(pr.astype(BF16), v) / jnp.sum(pr, axis=-1, keepdims=True)
        o_ref[:, cols] = o.astype(o_ref.dtype)


def _interleaved_masks(t_rows, total, past, n_heads, strict):
    qpos = past + lax.broadcasted_iota(jnp.int32, (t_rows, total), 0)
    col = lax.broadcasted_iota(jnp.int32, (t_rows, total), 1)
    kpos = col // n_heads
    causal = (kpos < qpos) if strict else (kpos <= qpos)
    return col % n_heads, causal


def _fox_sample_kernel(q_ref, kc_ref, vc_ref, kn_ref, vn_ref, fq_ref, fk_ref, o_ref, *, past, total, n_heads):
    t_rows = o_ref.shape[0]
    k = _join_rows(kc_ref[...], kn_ref[...], total)
    v = _join_rows(vc_ref[...], vn_ref[...], total)
    khead, causal = _interleaved_masks(t_rows, total, past, n_heads, strict=False)
    fk = fk_ref[...]
    for hh in range(n_heads):
        cols = slice(hh * LANE, (hh + 1) * LANE)
        s = (fq_ref[:, hh:hh + 1] + fk) * LOG2E + _dot_nt(q_ref[:, cols], k)
        s = jnp.where((khead == hh) & causal, s, NEG_INF)
        mx = jnp.max(s, axis=-1, keepdims=True)
        pr = jnp.exp2(s - mx)
        o = _dot(pr.astype(BF16), v) / jnp.sum(pr, axis=-1, keepdims=True)
        o_ref[:, cols] = o.astype(o_ref.dtype)


def _sb_sample_kernel(q_ref, kc_ref, vc_ref, kn_ref, vn_ref, o_ref, *, past, total, n_heads):
    t_rows = o_ref.shape[0]
    nb = total // LANE
    k = _join_rows(kc_ref[...], kn_ref[...], total)
    v = _join_rows(vc_ref[...], vn_ref[...], total)
    khead, causal = _interleaved_masks(t_rows, total, past, n_heads, strict=True)
    tri = _suffix_tri(LANE)
    for hh in range(n_heads):
        cols = slice(hh * LANE, (hh + 1) * LANE)
        ok = (khead == hh) & causal
        z = _dot_nt(q_ref[:, cols], k)
        log_beta = _log_sigmoid(z)
        log_fail = jnp.where(ok, log_beta - z, 0.0)
        stacked = jnp.concatenate([log_fail[:, c * LANE:(c + 1) * LANE] for c in range(nb)], axis=0)
        hi, lo = _split2(stacked)
        incl = _dot(hi, tri) + _dot(lo, tri)
        across = jnp.zeros((t_rows, 1), F32)
        w_blocks = [None] * nb
        for c in reversed(range(nb)):
            sl = slice(c * LANE, (c + 1) * LANE)
            inc = incl[c * t_rows:(c + 1) * t_rows]
            w = jnp.exp(log_beta[:, sl] + (inc - log_fail[:, sl] + across))
            w_blocks[c] = jnp.where(ok[:, sl], w, 0.0).astype(BF16)
            across = across + inc[:, 0:1]
        o_ref[:, cols] = _dot(jnp.concatenate(w_blocks, axis=-1), v).astype(o_ref.dtype)


def _sample_specs(n_streams, t_rows, past, gw):
    row = lambda b: (b, 0)
    return [pl.BlockSpec((t_rows, gw), row), pl.BlockSpec((past, gw), row), pl.BlockSpec((past, gw), row),
            pl.BlockSpec((t_rows, gw), row), pl.BlockSpec((t_rows, gw), row)]


def _pad_keys(n):
    return -(-n // LANE) * LANE


def _band_sample(q, kc, vc, kn, vn, table, n_streams):
    m, gw = q.shape
    t_rows, past = m // n_streams, kc.shape[0] // n_streams
    total = table.shape[-1]
    return pl.pallas_call(
        functools.partial(_band_sample_kernel, total=total),
        grid=(n_streams,),
        in_specs=_sample_specs(n_streams, t_rows, past, gw) + [_resident(table.shape)],
        out_specs=pl.BlockSpec((t_rows, gw), lambda b: (b, 0)),
        out_shape=jax.ShapeDtypeStruct((m, gw), BF16),
        compiler_params=_cparams("parallel"), name="band_sample",
    )(q, kc, vc, kn, vn, table)


def _interleaved_specs(t_rows, past, gw, n_heads, first_stream):
    row = lambda b: (b, 0)
    cache = pl.BlockSpec((past * n_heads, LANE), lambda b: (first_stream + b, 0))
    new = pl.BlockSpec((t_rows * n_heads, LANE), row)
    return [pl.BlockSpec((t_rows, gw), row), cache, cache, new, new]


def _fox_sample(q, kc, vc, kn, vn, f_rows, f_cols, n_streams, past, layer):
    m, gw = q.shape
    t_rows = m // n_streams
    total = f_cols.shape[-1]
    return pl.pallas_call(
        functools.partial(_fox_sample_kernel, past=past, total=total, n_heads=L_HEADS),
        grid=(n_streams,),
        in_specs=_interleaved_specs(t_rows, past, gw, L_HEADS, layer * n_streams) + [
            pl.BlockSpec((t_rows, LANE), lambda b: (b, 0)),
            pl.BlockSpec((None, 1, total), lambda b: (b, 0, 0))],
        out_specs=pl.BlockSpec((t_rows, gw), lambda b: (b, 0)),
        out_shape=jax.ShapeDtypeStruct((m, gw), BF16),
        compiler_params=_cparams("parallel"), name="fox_sample",
    )(q, kc, vc, kn, vn, f_rows, f_cols)


def _sb_sample(q, kc, vc, kn, vn, n_streams, past, layer):
    m, gw = q.shape
    t_rows = m // n_streams
    total = _pad_keys((past + t_rows) * L_HEADS)
    return pl.pallas_call(
        functools.partial(_sb_sample_kernel, past=past, total=total, n_heads=L_HEADS),
        grid=(n_streams,),
        in_specs=_interleaved_specs(t_rows, past, gw, L_HEADS, layer * n_streams),
        out_specs=pl.BlockSpec((t_rows, gw), lambda b: (b, 0)),
        out_shape=jax.ShapeDtypeStruct((m, gw), BF16),
        compiler_params=_cparams("parallel"), name="sb_sample",
    )(q, kc, vc, kn, vn)


def _mla_sample(qd, kd_c, vd_c, kd_n, vd_n, n_streams):
    n_heads, m, dk = qd.shape
    gw = vd_n.shape[1]
    t_rows, past = m // n_streams, vd_c.shape[0] // n_streams
    total = _pad_keys(past + t_rows)
    lat = lambda rows: pl.BlockSpec((n_heads, rows, dk), lambda b: (0, b, 0))
    row = lambda b: (b, 0)
    return pl.pallas_call(
        functools.partial(_mla_sample_kernel, past=past, total=total, n_heads=n_heads),
        grid=(n_streams,),
        in_specs=[lat(t_rows), lat(past), pl.BlockSpec((past, gw), row), lat(t_rows),
                  pl.BlockSpec((t_rows, gw), row)],
        out_specs=pl.BlockSpec((t_rows, gw), row),
        out_shape=jax.ShapeDtypeStruct((m, gw), BF16),
        compiler_params=_cparams("parallel"), name="mla_sample",
    )(qd, kd_c, vd_c, kd_n, vd_n)


def _out_proj_kernel(oa_ref, ob_ref, oc_ref, od_ref, gmix_ref, wo_ref, x_ref, gpost_ref, gffn_ref,
                     x1_ref, h2_ref):
    gw = oa_ref.shape[1]
    y = None
    for gi, o_ref in enumerate((oa_ref, ob_ref, oc_ref, od_ref)):
        n = _rms(o_ref[...].astype(F32), gmix_ref[gi:gi + 1, :]).astype(BF16)
        part = _dot(n, wo_ref[gi * gw:(gi + 1) * gw, :])
        y = part if y is None else y + part
    x1 = x_ref[...] + _rms(y, gpost_ref[...])
    x1_ref[...] = x1
    h2_ref[...] = _rms(x1, gffn_ref[...]).astype(BF16)


def _out_proj(outs, g_mix, w_o, x, g_post, g_ffn, tm):
    m, d = x.shape
    gw = d // 4
    rows = lambda i: (i, 0)
    return pl.pallas_call(
        _out_proj_kernel, grid=(m // tm,),
        in_specs=[pl.BlockSpec((tm, gw), rows)] * 4 + [_resident(g_mix.shape), _resident(w_o.shape),
                                                       pl.BlockSpec((tm, d), rows),
                                                       _resident((1, d)), _resident((1, d))],
        out_specs=[pl.BlockSpec((tm, d), rows), pl.BlockSpec((tm, d), rows)],
        out_shape=[jax.ShapeDtypeStruct((m, d), F32), jax.ShapeDtypeStruct((m, d), BF16)],
        compiler_params=_cparams("parallel"), name="out_proj",
    )(*outs, g_mix, w_o, x, g_post, g_ffn)


FFN_ROW_GROUP = 256


def _ffn_kernel(h_ref, wg_ref, wu_ref, wd_ref, x_ref, g_ref, o_ref, acc_ref):
    f = pl.program_id(1)

    @pl.when(f == 0)
    def _():
        acc_ref[...] = jnp.zeros_like(acc_ref)

    tm = h_ref.shape[0]
    rg = min(FFN_ROW_GROUP, tm)

    def gate_up(r):
        h = h_ref[r * rg:(r + 1) * rg, :]
        return _dot(h, wg_ref[...]), _dot(h, wu_ref[...])

    pending = [gate_up(0)]
    for r in range(tm // rg):
        if r + 1 < tm // rg:
            pending.append(gate_up(r + 1))
        gate, up = pending.pop(0)
        act = (gate * jax.nn.sigmoid(gate) * up).astype(BF16)
        acc_ref[r * rg:(r + 1) * rg, :] += _dot(act, wd_ref[...])

    @pl.when(f == pl.num_programs(1) - 1)
    def _():
        o_ref[...] = x_ref[...] + _rms(acc_ref[...], g_ref[...])


def _ffn(h2, w_gate, w_up, w_down, x1, g, tm, tf):
    m, d = x1.shape
    ff = w_gate.shape[1]
    rows = lambda i, f: (i, 0)
    once_per_row_block = pl.BlockSpec((tm, d), rows, pipeline_mode=pl.Buffered(1))
    return pl.pallas_call(
        _ffn_kernel, grid=(m // tm, ff // tf),
        in_specs=[once_per_row_block, pl.BlockSpec((d, tf), lambda i, f: (0, f)),
                  pl.BlockSpec((d, tf), lambda i, f: (0, f)), pl.BlockSpec((tf, d), lambda i, f: (f, 0)),
                  once_per_row_block, pl.BlockSpec((1, d), lambda i, f: (0, 0))],
        out_specs=pl.BlockSpec((tm, d), rows),
        out_shape=jax.ShapeDtypeStruct((m, d), F32),
        scratch_shapes=[pltpu.VMEM((tm, d), F32)],
        compiler_params=_cparams("parallel", "arbitrary"), name="ffn",
    )(h2, w_gate, w_up, w_down, x1, g)


def _pad_cols(w, n):
    return jnp.pad(w, ((0, 0), (0, n - w.shape[1])))


def _swap_halves(w):
    half = w.shape[1] // 2
    return jnp.concatenate([w[:, half:], w[:, :half]], axis=1)


def _layer_weights(l, d, w_in, b_f, g_q_lat, g_kv_lat, w_uq, w_uk, w_uv):
    gw = d // 4
    q_rank, kv_rank = g_q_lat.shape[1], g_kv_lat.shape[1]
    sizes = (3 * gw, 3 * gw, L_HEADS, 3 * gw, q_rank, kv_rank, ROPE_DIM)
    za, zb, zf, zc, dq, dkv, dkr = jnp.split(w_in[l], np.cumsum(sizes)[:-1].tolist(), axis=1)
    w_all = jnp.concatenate([za, zb, zc, dq, dkv, _pad_cols(dkr, LANE), _pad_cols(_swap_halves(dkr), LANE),
                             _pad_cols(zf, LANE)], axis=1).astype(BF16)
    hd = NOPE_DIM + ROPE_DIM
    uq = w_uq[l]
    nope = [uq[:, h * hd:h * hd + NOPE_DIM] for h in range(L_HEADS)]
    rope = [uq[:, h * hd + NOPE_DIM:(h + 1) * hd] for h in range(L_HEADS)]
    w_q = jnp.concatenate(nope + [_pad_cols(r, LANE) for r in rope], axis=1).astype(BF16)
    w_qs = jnp.concatenate([_pad_cols(_swap_halves(r), LANE) for r in rope], axis=1).astype(BF16)
    return dict(w_in=w_all, b_f=_pad_cols(b_f[l][None, :].astype(F32), LANE), g_q=g_q_lat[l][None, :],
                g_kv=g_kv_lat[l][None, :], w_q=w_q, w_qs=w_qs,
                w_uk=w_uk[l].astype(BF16), w_uv=w_uv[l].astype(BF16))


def _rope_tables(pos):
    half = ROPE_DIM // 2
    inv_freq = ROPE_THETA ** (-jnp.arange(half, dtype=F32) / half)
    ang = pos.astype(F32)[:, None] * inv_freq[None, :]
    cos, sin = jnp.cos(ang), jnp.sin(ang)
    zero = jnp.zeros((pos.shape[0], LANE - ROPE_DIM), F32)
    return jnp.concatenate([cos, cos, zero], axis=1), jnp.concatenate([-sin, sin, zero], axis=1)


def kernel(x_prompt, x_sample, cache_a_k, cache_a_v, cache_b_k, cache_b_v, cache_b_logf, cache_c_k, cache_c_v,
           cache_d_ckv, cache_d_kpe, w_in, b_f, rel_bias, g_q_lat, g_kv_lat, w_uq, w_uk, w_uv, g_mix, w_o,
           g_pre_attn, g_post_attn, g_pre_ffn, g_post_ffn, w_gate, w_up, w_down):
    batch, seq, d = x_prompt.shape
    n_streams, t_rows, _ = x_sample.shape
    depth = w_in.shape[0]
    past = cache_b_k.shape[2]
    a_win = cache_a_k.shape[2]
    gw = d // 4
    mp, ms = batch * seq, n_streams * t_rows
    bq, rb = 512, 128
    tm_p, tm_s = 256, ms

    cos_p, sin_p = _rope_tables(jnp.arange(seq))
    cos_s, sin_s = _rope_tables(jnp.tile(past + jnp.arange(t_rows), n_streams))

    cache_ckv = cache_d_ckv.reshape(depth * n_streams * past, -1)
    cache_kpe128 = _pad_cols(cache_d_kpe.reshape(depth * n_streams * past, ROPE_DIM), LANE)
    xp = x_prompt.reshape(mp, d)
    xs = x_sample.reshape(ms, d)
    p_states, s_states = [], []
    stacked = None
    for l in range(depth):
        lw = _layer_weights(l, d, w_in, b_f, g_q_lat, g_kv_lat, w_uq, w_uk, w_uv)
        g_attn = g_pre_attn[l][None, :]
        wo = w_o[l].astype(BF16)
        wg, wu, wd = w_gate[l].astype(BF16), w_up[l].astype(BF16), w_down[l].astype(BF16)
        g_post, g_ffn, g_out = g_post_attn[l][None, :], g_pre_ffn[l][None, :], g_post_ffn[l][None, :]

        pr = _project(xp, g_attn, lw, cos_p, sin_p, tm_p, stack=(l, depth, stacked))
        stacked = {name: pr[name] for name in _STACKED_STATES}
        f_run = _cumsum(pr['logf'], seq)
        f_cols = (-f_run[:, :L_HEADS]).reshape(batch, seq, L_HEADS).transpose(0, 2, 1)
        f_cols = f_cols.reshape(batch, L_HEADS, seq // bq, bq)
        bq_flash = 2 * bq if seq % (2 * bq) == 0 else bq
        kd, vd = _mla_keys(pr['ckv'], pr['kpe128'], lw['w_uk'], lw['w_uv'], 512)
        oa = _band_prompt(pr['qa'], pr['ka16'], pr['va16'], _band_table(rel_bias[l], rb), batch, seq, bq, rb)
        ob = _fox_prompt(pr['qb'], pr['kb16'], pr['vb16'], f_run, f_cols, batch, seq, bq_flash, bq, 2 * rb, 2)
        oc = _sb_prompt(pr['qc'], pr['kc16'], pr['vc16'], batch, seq, bq, rb)
        od = _mla_prompt(pr['qd'], kd, vd, batch, seq, bq_flash, bq, 4 * rb, 1)
        x1, h2 = _out_proj((oa, ob, oc, od), g_mix[l], wo, xp, g_post, g_ffn, tm_p)
        xp = _ffn(h2, wg, wu, wd, x1, g_out, 1024 if mp % 1024 == 0 else 512, 512)
        keep = min(a_win, seq)
        p_states.append((
            pr['ka'].reshape(batch, seq, gw)[:, seq - keep:].reshape(batch, keep, A_HEADS, gw // A_HEADS),
            pr['va'].reshape(batch, seq, gw)[:, seq - keep:].reshape(batch, keep, A_HEADS, gw // A_HEADS),
            pr['logf'][:, :L_HEADS].reshape(batch, seq, L_HEADS),
            pr['ckv'].reshape(batch, seq, -1), pr['kpe'].reshape(batch, seq, ROPE_DIM)))

        sr = _project(xs, g_attn, lw, cos_s, sin_s, tm_s)
        qa, ka, va, qb, kb, vb, qc, kc, vc = (sr[n] for n in ('qa', 'ka', 'va', 'qb', 'kb', 'vb', 'qc', 'kc', 'vc'))
        logf, qd, ckv, kpe, kpe128 = (sr[n] for n in ('logf', 'qd', 'ckv', 'kpe', 'kpe128'))
        tot_a = _pad_keys(a_win + t_rows)
        rel = (a_win + np.arange(t_rows))[:, None] - np.arange(tot_a)[None, :]
        tab = rel_bias[l][:, np.clip(rel, -REL_CLIP, REL_CLIP) + REL_CLIP].astype(F32)
        tab = jnp.where(jnp.asarray(np.arange(tot_a) < a_win + t_rows)[None, None, :], tab, NEG_INF)
        oa = _band_sample(qa, cache_a_k[l].reshape(n_streams * a_win, gw),
                          cache_a_v[l].reshape(n_streams * a_win, gw), ka, va, tab, n_streams)
        f_new = _cumsum(logf, t_rows)
        cl = _pad_cols(cache_b_logf[l].reshape(n_streams * past, L_HEADS).astype(F32), LANE)
        suffix = _cumsum(cl, past, reverse=True)
        n_keys = (past + t_rows) * L_HEADS
        f_cols = jnp.concatenate([suffix[:, :L_HEADS].reshape(n_streams, past * L_HEADS),
                                  -f_new[:, :L_HEADS].reshape(n_streams, t_rows * L_HEADS),
                                  jnp.zeros((n_streams, _pad_keys(n_keys) - n_keys), F32)], axis=1)[:, None, :]
        rows_of = lambda a: a.reshape(-1, LANE)
        ob = _fox_sample(qb, rows_of(cache_b_k), rows_of(cache_b_v), rows_of(kb), rows_of(vb),
                         f_new, f_cols, n_streams, past, l)
        oc = _sb_sample(qc, rows_of(cache_c_k), rows_of(cache_c_v), rows_of(kc), rows_of(vc), n_streams, past, l)
        kd_n, vd_n = _mla_keys(ckv, kpe128, lw['w_uk'], lw['w_uv'], ms)
        kd_c, vd_c = _mla_keys(cache_ckv, cache_kpe128, lw['w_uk'], lw['w_uv'], 512,
                               m=n_streams * past, first_block=l * n_streams * past // 512)
        od = _mla_sample(qd, kd_c, vd_c, kd_n, vd_n, n_streams)
        x1, h2 = _out_proj((oa, ob, oc, od), g_mix[l], wo, xs, g_post, g_ffn, tm_s)
        xs = _ffn(h2, wg, wu, wd, x1, g_out, ms, 512)
        s_states.append((
            ka.reshape(n_streams, t_rows, A_HEADS, gw // A_HEADS),
            va.reshape(n_streams, t_rows, A_HEADS, gw // A_HEADS),
            kb.reshape(n_streams, t_rows, L_HEADS, LANE), vb.reshape(n_streams, t_rows, L_HEADS, LANE),
            logf[:, :L_HEADS].reshape(n_streams, t_rows, L_HEADS),
            kc.reshape(n_streams, t_rows, L_HEADS, LANE), vc.reshape(n_streams, t_rows, L_HEADS, LANE),
            ckv.reshape(n_streams, t_rows, -1), kpe.reshape(n_streams, t_rows, ROPE_DIM)))

    p_a_k, p_a_v, p_b_logf, p_d_ckv, p_d_kpe = [jnp.stack(s) for s in zip(*p_states)]
    p_b_k, p_b_v, p_c_k, p_c_v = (stacked[name].reshape(depth, batch, seq, L_HEADS, LANE)
                                  for name in _STACKED_STATES)
    s_out = [jnp.stack(s) for s in zip(*s_states)]
    return (xp.reshape(batch, seq, d), xs.reshape(n_streams, t_rows, d),
            p_a_k, p_a_v, p_b_k, p_b_v, p_b_logf, p_c_k, p_c_v, p_d_ckv, p_d_kpe, *s_out)
```

```python
import functools
import math

import numpy as np
import jax
import jax.numpy as jnp
from jax import lax
from jax.experimental import pallas as pl
from jax.experimental.pallas import tpu as pltpu

CHUNK = 64
A_HEADS = 8
L_HEADS = 4
BAND_CHUNKS = 8
REL_CLIP = 128
NOPE_DIM = 128
ROPE_DIM = 64
ROPE_THETA = 10000.0
RMS_EPS = 1e-6
NEG_INF = -1e30
LOG2E = math.log2(math.e)

LANE = 128
MXU_DEPTH = 256
VMEM_LIMIT = 56 * 1024 * 1024

F32 = jnp.float32
BF16 = jnp.bfloat16


def _cparams(*sem):
    return pltpu.CompilerParams(dimension_semantics=sem, vmem_limit_bytes=VMEM_LIMIT)


def _resident(shape):
    return pl.BlockSpec(shape, lambda *_: (0,) * len(shape), pipeline_mode=pl.Buffered(1))


def _rms(x, g):
    return x * lax.rsqrt(jnp.mean(x * x, axis=-1, keepdims=True) + RMS_EPS) * g


def _log_sigmoid(x):
    return jnp.minimum(x, 0.0) - jnp.log1p(jnp.exp(-jnp.abs(x)))


def _dot(a, b):
    return jnp.dot(a, b, preferred_element_type=F32)


def _dot_nt(a, b):
    return lax.dot_general(a, b, (((1,), (1,)), ((), ())), preferred_element_type=F32)


def _split2(x):
    hi = x.astype(BF16)
    lo = (x - hi.astype(F32)).astype(BF16)
    return hi, lo


def _split3(x):
    hi = x.astype(BF16)
    r = x - hi.astype(F32)
    mid = r.astype(BF16)
    lo = (r - mid.astype(F32)).astype(BF16)
    return hi, mid, lo


_PROJ_INPUTS = 13


def _latent_keys(ckv, kpe, wuk_ref, wuv_ref, kd_ref, vd_ref):
    c = ckv.astype(BF16)
    kn = _dot(c, wuk_ref[...])
    vd_ref[...] = _dot(c, wuv_ref[...]).astype(BF16)
    kp = kpe.astype(BF16)
    for hh in range(kd_ref.shape[0]):
        kd_ref[hh, :, 0:LANE] = kn[:, LANE * hh:LANE * (hh + 1)].astype(BF16)
        kd_ref[hh, :, LANE:2 * LANE] = kp


def _proj_kernel(*refs, gw, q_rank, kv_rank, rope_dim, q_scales):
    (x_ref, g_ref, wab_ref, wrest_ref, bf_ref, gq_ref, gkv_ref, wq_ref, wqs_ref, wuk_ref, wuv_ref, cos_ref, sin_ref,
     ) = refs[:_PROJ_INPUTS]
    (qa_ref, ka_ref, va_ref, ka16_ref, va16_ref, qb_ref, kb_ref, vb_ref, kb16_ref, vb16_ref,
     qc_ref, kc_ref, vc_ref, kc16_ref, vc16_ref, logf_ref, qd_ref, ckv_ref, kpe_ref, kd_ref, vd_ref,
     ) = refs[len(refs) - len(_PROJ_OUTPUTS):]
    h = _rms(x_ref[...], g_ref[...]).astype(BF16)

    n_ab = wab_ref.shape[0]
    z_rest = _dot_nt(h, wrest_ref[...])
    zf = z_rest[:, 0:LANE]
    z_rest = pltpu.roll(z_rest, z_rest.shape[1] - L_HEADS, 1)

    def seg(a, n):
        if a < n_ab:
            return _dot_nt(h, wab_ref[a:a + n, :])
        return z_rest[:, a - n_ab:a - n_ab + n]

    groups = ((qa_ref, ka_ref, va_ref, ka16_ref, va16_ref), (qb_ref, kb_ref, vb_ref, kb16_ref, vb16_ref),
              (qc_ref, kc_ref, vc_ref, kc16_ref, vc16_ref))
    for gi, (q_r, k_r, v_r, k16_r, v16_r) in enumerate(groups):
        base = 3 * gw * gi
        q_r[...] = (seg(base, gw) * q_scales[gi]).astype(BF16)
        k = seg(base + gw, gw)
        k16_r[...] = k.astype(BF16)
        v = seg(base + 2 * gw, gw)
        v16_r[...] = v.astype(BF16)
        for state_ref, val in ((k_r, k), (v_r, v)):
            if state_ref.shape[-1] == gw:
                state_ref[...] = val
            else:
                n_h = gw // LANE
                for hh in range(n_h):
                    state_ref[pl.ds(hh, val.shape[0], stride=n_h), :] = val[:, hh * LANE:(hh + 1) * LANE]
    off = 9 * gw
    dq = seg(off, q_rank)
    off += q_rank
    dkv = seg(off, kv_rank)
    off += kv_rank
    dkr = seg(off, LANE)
    half = rope_dim // 2
    lane = lax.broadcasted_iota(jnp.int32, dkr.shape, 1)
    dkr_sw = jnp.where(lane < half, pltpu.roll(dkr, LANE - half, 1), pltpu.roll(dkr, half, 1))

    logf_ref[...] = _log_sigmoid(zf + bf_ref[...])
    ckv = _rms(dkv, gkv_ref[...])
    ckv_ref[...] = ckv
    cos = cos_ref[...]
    sin = sin_ref[...]
    kpe = dkr * cos + dkr_sw * sin
    kpe_ref[...] = kpe[:, :rope_dim]
    _latent_keys(ckv, kpe, wuk_ref, wuv_ref, kd_ref, vd_ref)

    nq = _rms(dq, gq_ref[...]).astype(BF16)
    qf = _dot(nq, wq_ref[...])
    qs = _dot(nq, wqs_ref[...])
    n_heads = qd_ref.shape[0]
    for hh in range(n_heads):
        qd_ref[hh, :, 0:LANE] = (qf[:, LANE * hh:LANE * (hh + 1)] * q_scales[3]).astype(BF16)
        r0 = LANE * (n_heads + hh)
        rot = qf[:, r0:r0 + LANE] * cos + qs[:, LANE * hh:LANE * (hh + 1)] * sin
        qd_ref[hh, :, LANE:2 * LANE] = (rot * q_scales[3]).astype(BF16)


def _q_scales(gw):
    return ((gw // A_HEADS) ** -0.5 * LOG2E, (gw // L_HEADS) ** -0.5 * LOG2E, (gw // L_HEADS) ** -0.5,
            (NOPE_DIM + ROPE_DIM) ** -0.5 * LOG2E)


_PROJ_OUTPUTS = ('qa', 'ka', 'va', 'ka16', 'va16', 'qb', 'kb', 'vb', 'kb16', 'vb16',
                 'qc', 'kc', 'vc', 'kc16', 'vc16', 'logf', 'qd', 'ckv', 'kpe', 'kd', 'vd')


_STACKED_STATES = ('kb', 'vb', 'kc', 'vc')


def _project(x, g, lw, cos_t, sin_t, tm, stack=None):
    m, d = x.shape
    gw = d // 4
    q_rank, kv_rank = lw['g_q'].shape[1], lw['g_kv'].shape[1]
    n_pos_blocks = cos_t.shape[0] // tm
    rows = lambda i: (i, 0)
    pos = lambda i: (i % n_pos_blocks, 0)
    f32o = lambda n: jax.ShapeDtypeStruct((m, n), F32)
    bf16o = lambda n: jax.ShapeDtypeStruct((m, n), BF16)
    blk = lambda n: pl.BlockSpec((tm, n), rows)
    group = [bf16o(gw), f32o(gw), f32o(gw), bf16o(gw), bf16o(gw)]
    per_head = jax.ShapeDtypeStruct((L_HEADS, m, 2 * LANE), BF16)
    per_head_blk = pl.BlockSpec((L_HEADS, tm, 2 * LANE), lambda i: (0, i, 0))
    out_shape = group * 3 + [f32o(LANE), per_head, f32o(kv_rank), f32o(ROPE_DIM), per_head, bf16o(gw)]
    out_specs = [blk(gw)] * 15 + [blk(LANE), per_head_blk, blk(kv_rank), blk(ROPE_DIM), per_head_blk, blk(gw)]
    in_specs = [blk(d), _resident((1, d)), _resident(lw['w_ab'].shape), _resident(lw['w_rest'].shape),
                _resident((1, LANE)),
                _resident((1, q_rank)), _resident((1, kv_rank)),
                _resident(lw['w_q'].shape), _resident(lw['w_qs'].shape),
                _resident(lw['w_uk'].shape), _resident(lw['w_uv'].shape),
                pl.BlockSpec((tm, LANE), pos), pl.BlockSpec((tm, LANE), pos)]
    args = [x, g, lw['w_ab'], lw['w_rest'], lw['b_f'], lw['g_q'], lw['g_kv'], lw['w_q'], lw['w_qs'],
            lw['w_uk'], lw['w_uv'], cos_t, sin_t]
    aliases = {}
    if stack is not None:
        layer, depth, buffers = stack
        for name in _STACKED_STATES:
            o = _PROJ_OUTPUTS.index(name)
            n_h = gw // LANE
            out_shape[o] = jax.ShapeDtypeStruct((depth, m * n_h, LANE), F32)
            out_specs[o] = pl.BlockSpec((None, tm * n_h, LANE), lambda i: (layer, i, 0))
            if buffers is not None:
                aliases[len(args)] = o
                args.append(buffers[name])
                in_specs.append(pl.BlockSpec(memory_space=pl.ANY))
    kern = functools.partial(_proj_kernel, gw=gw, q_rank=q_rank, kv_rank=kv_rank, rope_dim=ROPE_DIM,
                             q_scales=_q_scales(gw))
    outs = pl.pallas_call(
        kern, grid=(m // tm,), in_specs=in_specs, out_specs=out_specs, out_shape=out_shape,
        input_output_aliases=aliases, compiler_params=_cparams("parallel"), name="in_proj",
    )(*args)
    return dict(zip(_PROJ_OUTPUTS, outs))


def _mla_keys_kernel(ckv_ref, kpe_ref, wuk_ref, wuv_ref, kd_ref, vd_ref):
    _latent_keys(ckv_ref[...], kpe_ref[...], wuk_ref, wuv_ref, kd_ref, vd_ref)


def _mla_keys(ckv, kpe128, w_uk, w_uv, tm, m=None, first_block=0):
    r = ckv.shape[1]
    m = ckv.shape[0] if m is None else m
    n = w_uk.shape[1]
    rows = lambda i: (i, 0)
    src = lambda i: (first_block + i, 0)
    return pl.pallas_call(
        _mla_keys_kernel, grid=(m // tm,),
        in_specs=[pl.BlockSpec((tm, r), src), pl.BlockSpec((tm, LANE), src),
                  _resident(w_uk.shape), _resident(w_uv.shape)],
        out_specs=[pl.BlockSpec((L_HEADS, tm, 2 * LANE), lambda i: (0, i, 0)), pl.BlockSpec((tm, n), rows)],
        out_shape=[jax.ShapeDtypeStruct((L_HEADS, m, 2 * LANE), BF16), jax.ShapeDtypeStruct((m, n), BF16)],
        compiler_params=_cparams("parallel"), name="mla_keys",
    )(ckv, kpe128, w_uk, w_uv)


def _cumsum_kernel(x_ref, o_ref, carry_ref, *, seg, reverse):
    @pl.when(pl.program_id(1) == 0)
    def _():
        carry_ref[...] = jnp.zeros_like(carry_ref)

    x = x_ref[...]
    tc = x.shape[0]
    r = lax.broadcasted_iota(jnp.int32, (tc, tc), 0)
    c = lax.broadcasted_iota(jnp.int32, (tc, tc), 1)
    keep = (c > r) if reverse else (c <= r)
    if seg < tc:
        keep = keep & (r // seg == c // seg)
    tri = jnp.where(keep, 1.0, 0.0).astype(BF16)
    hi, mid, lo = _split3(x)
    y = _dot(tri, hi) + _dot(tri, mid) + _dot(tri, lo)
    o_ref[...] = y + carry_ref[...]
    if seg > tc:
        carry_ref[...] += jnp.sum(x, axis=0, keepdims=True)


def _cumsum(x, seg, reverse=False):
    m, n = x.shape
    tc = min(512, m) if seg >= 512 else min(256, m)
    nb = max(seg // tc, 1)
    nseg = m // (nb * tc)
    if reverse:
        idx = lambda s, j: (s * nb + nb - 1 - j, 0)
    else:
        idx = lambda s, j: (s * nb + j, 0)
    return pl.pallas_call(
        functools.partial(_cumsum_kernel, seg=seg, reverse=reverse), grid=(nseg, nb),
        in_specs=[pl.BlockSpec((tc, n), idx)], out_specs=pl.BlockSpec((tc, n), idx),
        out_shape=jax.ShapeDtypeStruct((m, n), F32),
        scratch_shapes=[pltpu.VMEM((1, n), F32)],
        compiler_params=_cparams("parallel", "arbitrary"), name="cumsum",
    )(x)


def _lane_tile(x, n):
    return x if n == LANE else jnp.concatenate([x] * (n // LANE), axis=1)


FOX_UNDERFLOW = 152.0


def _flash_kernel(*refs, fox, bq, bk, rb, ahead):
    if fox:
        q_ref, k_ref, v_ref, fq_ref, fk_ref, o_ref, acc_ref, m_ref, l_ref, kmax_ref = refs
    else:
        q_ref, k_ref, v_ref, o_ref, acc_ref, m_ref, l_ref = refs
    head = pl.program_id(1)
    qi = pl.program_id(2)
    m_ref[...] = jnp.full_like(m_ref, NEG_INF)
    l_ref[...] = jnp.zeros_like(l_ref)
    acc_ref[...] = jnp.zeros_like(acc_ref)
    if fox:
        lane = lax.broadcasted_iota(jnp.int32, fq_ref.shape, 1)
        fq = jnp.sum(jnp.where(lane == head, fq_ref[...], 0.0), axis=-1, keepdims=True) * LOG2E

    def logits(r, j):
        rows = slice(r * rb, (r + 1) * rb)
        s = _dot_nt(q_ref[rows, :], k_ref[pl.ds(pl.multiple_of(j * bk, bk), bk), :])
        if fox:
            s = fq[rows] + (fk_ref[pl.ds(j, 1), :] * LOG2E + s)
        return s

    def update(r, s, j):
        rows = slice(r * rb, (r + 1) * rb)
        m_prev = m_ref[rows, :]
        m_new = jnp.maximum(m_prev, jnp.max(s, axis=-1, keepdims=True))
        alpha = jnp.exp2(m_prev - m_new)
        p = jnp.exp2(s - _lane_tile(m_new, s.shape[1]))
        l_ref[rows, :] = alpha * l_ref[rows, :] + jnp.sum(p, axis=-1, keepdims=True)
        v = v_ref[pl.ds(pl.multiple_of(j * bk, bk), bk), :]
        acc_ref[rows, :] = alpha * acc_ref[rows, :] + _dot(p.astype(BF16), v)
        m_ref[rows, :] = m_new

    def run(items):
        pending = [logits(r, j) for r, j, _ in items[:ahead]]
        for i, (r, j, diag) in enumerate(items):
            if i + ahead < len(items):
                pending.append(logits(*items[i + ahead][:2]))
            s = pending.pop(0)
            if diag is not None:
                row = r * rb + lax.broadcasted_iota(jnp.int32, s.shape, 0)
                col = diag + lax.broadcasted_iota(jnp.int32, s.shape, 1)
                ok = (col <= row) if fox else (col // CHUNK <= row // CHUNK)
                s = jnp.where(ok, s, NEG_INF)
            update(r, s, j)

    n_groups = bq // rb
    kb_per_q = bq // bk

    def full_block(j, carry):
        run([(r, j, None) for r in range(n_groups)])
        return carry

    items = []
    for kb in range(kb_per_q):
        first = kb * bk // rb
        items += [(r, qi * kb_per_q + kb, kb * bk) for r in range(first, first + bk // rb)]
        items += [(r, qi * kb_per_q + kb, None) for r in range(first + bk // rb, n_groups)]
    n_full = qi * kb_per_q
    if not fox:
        lax.fori_loop(0, n_full, full_block, 0)
        run(items)
    else:
        @pl.when(qi == 0)
        def _():
            def norm_max(c, best):
                kc = k_ref[pl.ds(pl.multiple_of(c * bk, bk), bk), :].astype(F32)
                return jnp.maximum(best, jnp.max(jnp.sum(kc * kc, axis=-1, keepdims=True)))
            kmax_ref[0] = jnp.sqrt(lax.fori_loop(0, k_ref.shape[0] // bk, norm_max, jnp.float32(0.0)))

        run(items)
        q32 = q_ref[...].astype(F32)
        row_bound = jnp.sqrt(jnp.sum(q32 * q32, axis=-1, keepdims=True)) * (kmax_ref[0] * 1.001) + fq + 1.0

        def worst(j):
            last_key = fk_ref[pl.ds(j, 1), :][:, bk - 1:bk] * LOG2E
            return jnp.max(row_bound + last_key - m_ref[:, 0:1])

        def more(carry):
            t, w = carry
            return jnp.logical_and(t < n_full, w >= -FOX_UNDERFLOW)

        def older_block(carry):
            t, _ = carry
            j = n_full - 1 - t
            full_block(j, 0)
            return t + 1, worst(jnp.maximum(j - 1, 0))

        lax.while_loop(more, older_block, (jnp.int32(0), worst(jnp.maximum(n_full - 1, 0))))
    o_ref[...] = (acc_ref[...] / l_ref[...]).astype(o_ref.dtype)


def _flash_scratch(bq):
    return [pltpu.VMEM((bq, LANE), F32), pltpu.VMEM((bq, LANE), F32), pltpu.VMEM((bq, LANE), F32)]


def _fox_prompt(q, k, v, f_rows, f_cols, batch, seq, bq, bk, rb, ahead):
    m, gw = q.shape
    nq = seq // bq
    qmap = lambda b, h, i: (b * nq + i, h)
    kvmap = lambda b, h, i: (b, h)
    return pl.pallas_call(
        functools.partial(_flash_kernel, fox=True, bq=bq, bk=bk, rb=rb, ahead=ahead),
        grid=(batch, L_HEADS, nq),
        in_specs=[pl.BlockSpec((bq, LANE), qmap), pl.BlockSpec((seq, LANE), kvmap),
                  pl.BlockSpec((seq, LANE), kvmap),
                  pl.BlockSpec((bq, LANE), lambda b, h, i: (b * nq + i, 0)),
                  pl.BlockSpec((None, None, seq // bk, bk), lambda b, h, i: (b, h, 0, 0))],
        out_specs=pl.BlockSpec((bq, LANE), qmap),
        out_shape=jax.ShapeDtypeStruct((m, gw), BF16),
        scratch_shapes=_flash_scratch(bq) + [pltpu.SMEM((1,), F32)],
        compiler_params=_cparams("parallel", "parallel", "arbitrary"), name="fox_prompt",
    )(q, k, v, f_rows, f_cols)


def _mla_prompt(qd, kd, vd, batch, seq, bq, bk, rb, ahead):
    n_heads, m, dk = qd.shape
    nq = seq // bq
    return pl.pallas_call(
        functools.partial(_flash_kernel, fox=False, bq=bq, bk=bk, rb=rb, ahead=ahead),
        grid=(batch, n_heads, nq),
        in_specs=[pl.BlockSpec((None, bq, dk), lambda b, h, i: (h, b * nq + i, 0)),
                  pl.BlockSpec((None, seq, dk), lambda b, h, i: (h, b, 0)),
                  pl.BlockSpec((seq, LANE), lambda b, h, i: (b, h))],
        out_specs=pl.BlockSpec((bq, LANE), lambda b, h, i: (b * nq + i, h)),
        out_shape=jax.ShapeDtypeStruct((m, n_heads * LANE), BF16),
        scratch_shapes=_flash_scratch(bq),
        compiler_params=_cparams("parallel", "parallel", "arbitrary"), name="mla_prompt",
    )(qd, kd, vd)


SB_UNDERFLOW = 105.0


def _suffix_tri(n):
    r = lax.broadcasted_iota(jnp.int32, (n, n), 0)
    c = lax.broadcasted_iota(jnp.int32, (n, n), 1)
    return jnp.where(r >= c, 1.0, 0.0).astype(BF16)


def _sb_kernel(q_ref, k_ref, v_ref, o_ref, acc_ref, across_ref, *, bq, rb, sub):
    qi = pl.program_id(2)
    acc_ref[...] = jnp.zeros_like(acc_ref)
    across_ref[...] = jnp.zeros_like(across_ref)
    tri = _suffix_tri(sub)

    n_groups = bq // rb

    def visit(start, diagonal):
        n_keys = [-(-((r + 1) * rb) // sub) * sub if diagonal else bq for r in range(n_groups)]

        def stage_logits(r):
            rows = slice(r * rb, (r + 1) * rb)
            return _dot_nt(q_ref[rows, :], k_ref[pl.ds(start, n_keys[r]), :])

        def stage_sums(r, z):
            log_fail = -jnp.maximum(z, 0.0) - jnp.log(1.0 + jnp.exp(-jnp.abs(z)))
            log_beta = log_fail + z
            ok = None
            if diagonal:
                row = r * rb + lax.broadcasted_iota(jnp.int32, z.shape, 0)
                col = lax.broadcasted_iota(jnp.int32, z.shape, 1)
                ok = col < row
                log_fail = jnp.where(ok, log_fail, 0.0)
            incl = {}
            for c in reversed(range(n_keys[r] // sub)):
                hi, lo = _split2(log_fail[:, c * sub:(c + 1) * sub])
                incl[c] = _dot(hi, tri) + _dot(lo, tri)
            return log_fail, log_beta, ok, incl

        def stage_values(r, state):
            log_fail, log_beta, ok, incl = state
            rows = slice(r * rb, (r + 1) * rb)
            across = across_ref[rows, :]
            acc = acc_ref[rows, :]
            for c in reversed(range(n_keys[r] // sub)):
                sl = slice(c * sub, (c + 1) * sub)
                w = jnp.exp(log_beta[:, sl] + (incl[c] - log_fail[:, sl] + _lane_tile(across, sub)))
                if diagonal:
                    w = jnp.where(ok[:, sl], w, 0.0)
                vb = v_ref[pl.ds(pl.multiple_of(start + c * sub, sub), sub), :]
                acc = acc + _dot(w.astype(BF16), vb)
                across = across + jnp.broadcast_to(incl[c][:, 0:1], across.shape)
            across_ref[rows, :] = across
            acc_ref[rows, :] = acc

        zs = [stage_logits(r) for r in range(n_groups)]
        states = [stage_sums(r, zs[r]) for r in range(min(2, n_groups))]
        for r in range(n_groups):
            stage_values(r, states[r])
            if r + 2 < n_groups:
                states.append(stage_sums(r + 2, zs[r + 2]))

    visit(pl.multiple_of(qi * bq, bq), True)

    def more(carry):
        t, worst = carry
        return jnp.logical_and(t < qi, worst > -SB_UNDERFLOW)

    def earlier_block(carry):
        t, _ = carry
        visit(pl.multiple_of((qi - 1 - t) * bq, bq), False)
        return t + 1, jnp.max(across_ref[...])

    lax.while_loop(more, earlier_block, (jnp.int32(0), jnp.max(across_ref[...])))
    o_ref[...] = acc_ref[...].astype(o_ref.dtype)


def _sb_prompt(q, k, v, batch, seq, bq, rb):
    m, gw = q.shape
    nq = seq // bq
    qmap = lambda b, h, i: (b * nq + i, h)
    kvmap = lambda b, h, i: (b, h)
    return pl.pallas_call(
        functools.partial(_sb_kernel, bq=bq, rb=rb, sub=MXU_DEPTH),
        grid=(batch, L_HEADS, nq),
        in_specs=[pl.BlockSpec((bq, LANE), qmap), pl.BlockSpec((seq, LANE), kvmap),
                  pl.BlockSpec((seq, LANE), kvmap)],
        out_specs=pl.BlockSpec((bq, LANE), qmap),
        out_shape=jax.ShapeDtypeStruct((m, gw), BF16),
        scratch_shapes=[pltpu.VMEM((bq, LANE), F32), pltpu.VMEM((bq, LANE), F32)],
        compiler_params=_cparams("parallel", "parallel", "arbitrary"), name="sb_prompt",
    )(q, k, v)


BAND_AHEAD = 2


def _band_kernel(q_ref, kp_ref, kc_ref, vp_ref, vc_ref, t_ref, o_ref, *, bq, rb):
    band = BAND_CHUNKS * CHUNK
    has_prev = pl.program_id(2) > 0
    first_half = lax.broadcasted_iota(jnp.int32, (rb, LANE), 1) < (LANE // 2)

    def band_cols(r):
        lo = bq - band + r * rb
        n_cur = (r + 1) * rb
        return lo, n_cur

    def stage_logits(r, hh):
        rows = slice(r * rb, (r + 1) * rb)
        lo, n_cur = band_cols(r)
        q = q_ref[rows, :]
        sel = first_half if hh == 0 else jnp.logical_not(first_half)
        qh = jnp.where(sel, q, jnp.zeros_like(q))
        return _dot_nt(qh, kc_ref[0:n_cur, :]), _dot_nt(qh, kp_ref[lo:bq, :])

    def stage_softmax(r, hh, raw):
        lo, n_cur = band_cols(r)
        n_prev = bq - lo
        sc = t_ref[hh, :, n_prev:n_prev + n_cur] + raw[0]
        sp = t_ref[hh, :, 0:n_prev] + raw[1]
        sp = jnp.where(has_prev, sp, NEG_INF)
        mx = jnp.maximum(jnp.max(sc, axis=-1, keepdims=True), jnp.max(sp, axis=-1, keepdims=True))
        pc = jnp.exp2(sc - mx)
        pp = jnp.exp2(sp - mx)
        den = jnp.sum(pc, axis=-1, keepdims=True) + jnp.sum(pp, axis=-1, keepdims=True)
        return (_dot(pc.astype(BF16), vc_ref[0:n_cur, :]) + _dot(pp.astype(BF16), vp_ref[lo:bq, :])) / den

    items = [(r, hh) for r in range(bq // rb) for hh in range(2)]
    pending = [stage_logits(*it) for it in items[:BAND_AHEAD]]
    out_first = None
    for i, (r, hh) in enumerate(items):
        if i + BAND_AHEAD < len(items):
            pending.append(stage_logits(*items[i + BAND_AHEAD]))
        out = stage_softmax(r, hh, pending.pop(0))
        if hh == 0:
            out_first = out
        else:
            o_ref[r * rb:(r + 1) * rb, :] = jnp.where(first_half, out_first, out).astype(o_ref.dtype)


def _band_table_kernel(g_ref, o_ref):
    rb, width = o_ref.shape
    x = jnp.broadcast_to(g_ref[...], (rb, g_ref.shape[-1]))
    y = pltpu.roll(x, 0, 1, stride=1, stride_axis=0)[:, rb:]
    i = lax.broadcasted_iota(jnp.int32, (rb, width), 0)
    j = lax.broadcasted_iota(jnp.int32, (rb, width), 1)
    first = (i // CHUNK) * CHUNK
    ok = (j >= first) & (j < first + (BAND_CHUNKS + 1) * CHUNK)
    o_ref[...] = jnp.where(ok, y * LOG2E, NEG_INF)


def _band_table(rel_bias, rb):
    band = BAND_CHUNKS * CHUNK
    width = band + rb
    rel = band + rb - np.arange(rb + width)
    g = rel_bias[:, np.clip(rel, -REL_CLIP, REL_CLIP) + REL_CLIP].astype(F32)[:, None, :]
    n_heads = rel_bias.shape[0]
    return pl.pallas_call(
        _band_table_kernel, grid=(n_heads,),
        in_specs=[pl.BlockSpec((None, 1, rb + width), lambda h: (h, 0, 0))],
        out_specs=pl.BlockSpec((None, rb, width), lambda h: (h, 0, 0)),
        out_shape=jax.ShapeDtypeStruct((n_heads, rb, width), F32),
        compiler_params=_cparams("parallel"), name="band_table",
    )(g)


def _band_prompt(q, k, v, table, batch, seq, bq, rb):
    assert bq >= BAND_CHUNKS * CHUNK and rb % CHUNK == 0
    m, gw = q.shape
    nq = seq // bq
    cur = lambda p, b, i: (b * nq + i, p)
    prev = lambda p, b, i: (b * nq + jnp.maximum(i - 1, 0), p)
    blk = lambda f: pl.BlockSpec((bq, LANE), f)
    return pl.pallas_call(
        functools.partial(_band_kernel, bq=bq, rb=rb),
        grid=(A_HEADS // 2, batch, nq),
        in_specs=[blk(cur), blk(prev), blk(cur), blk(prev), blk(cur),
                  pl.BlockSpec((2,) + table.shape[1:], lambda p, b, i: (p, 0, 0))],
        out_specs=blk(cur),
        out_shape=jax.ShapeDtypeStruct((m, gw), BF16),
        compiler_params=_cparams("parallel", "parallel", "parallel"), name="band_prompt",
    )(q, k, k, v, v, table)


def _join_rows(cache, new, total):
    pad = total - cache.shape[0] - new.shape[0]
    parts = [cache.astype(BF16), new.astype(BF16)]
    if pad:
        parts.append(jnp.zeros((pad, cache.shape[1]), BF16))
    return jnp.concatenate(parts, axis=0)


def _band_sample_kernel(q_ref, kc_ref, vc_ref, kn_ref, vn_ref, t_ref, o_ref, *, total):
    t_rows = q_ref.shape[0]
    first_half = lax.broadcasted_iota(jnp.int32, (t_rows, LANE), 1) < (LANE // 2)
    for p in range(A_HEADS // 2):
        cols = slice(p * LANE, (p + 1) * LANE)
        q = q_ref[:, cols]
        k = _join_rows(kc_ref[:, cols], kn_ref[:, cols], total)
        v = _join_rows(vc_ref[:, cols], vn_ref[:, cols], total)
        outs = []
        for hh in range(2):
            sel = first_half if hh == 0 else jnp.logical_not(first_half)
            qh = jnp.where(sel, q, jnp.zeros_like(q))
            s = t_ref[2 * p + hh] * LOG2E + _dot_nt(qh, k)
            mx = jnp.max(s, axis=-1, keepdims=True)
            pr = jnp.exp2(s - mx)
            outs.append(_dot(pr.astype(BF16), v) / jnp.sum(pr, axis=-1, keepdims=True))
        o_ref[:, cols] = jnp.where(first_half, outs[0], outs[1]).astype(o_ref.dtype)


def _mla_sample_kernel(q_ref, kc_ref, vc_ref, kn_ref, vn_ref, o_ref, *, past, total, n_heads):
    t_rows = o_ref.shape[0]
    qpos = past + lax.broadcasted_iota(jnp.int32, (t_rows, total), 0)
    kpos = lax.broadcasted_iota(jnp.int32, (t_rows, total), 1)
    ok = (kpos < past + t_rows) & (kpos // CHUNK <= qpos // CHUNK)
    for hh in range(n_heads):
        cols = slice(hh * LANE, (hh + 1) * LANE)
        k = _join_rows(kc_ref[hh], kn_ref[hh], total)
        v = _join_rows(vc_ref[:, cols], vn_ref[:, cols], total)
        s = jnp.where(ok, _dot_nt(q_ref[hh], k), NEG_INF)
        mx = jnp.max(s, axis=-1, keepdims=True)
        pr = jnp.exp2(s - mx)
        o = _dot(pr.astype(BF16), v) / jnp.sum(pr, axis=-1, keepdims=True)
        o_ref[:, cols] = o.astype(o_ref.dtype)


def _interleaved_masks(t_rows, total, past, n_heads, strict):
    qpos = past + lax.broadcasted_iota(jnp.int32, (t_rows, total), 0)
    col = lax.broadcasted_iota(jnp.int32, (t_rows, total), 1)
    kpos = col // n_heads
    causal = (kpos < qpos) if strict else (kpos <= qpos)
    return col % n_heads, causal


def _fox_sample_kernel(q_ref, kc_ref, vc_ref, kn_ref, vn_ref, fq_ref, fk_ref, o_ref, *, past, total, n_heads):
    t_rows = o_ref.shape[0]
    k = _join_rows(kc_ref[...], kn_ref[...], total)
    v = _join_rows(vc_ref[...], vn_ref[...], total)
    khead, causal = _interleaved_masks(t_rows, total, past, n_heads, strict=False)
    fk = fk_ref[...]
    for hh in range(n_heads):
        cols = slice(hh * LANE, (hh + 1) * LANE)
        s = (fq_ref[:, hh:hh + 1] + fk) * LOG2E + _dot_nt(q_ref[:, cols], k)
        s = jnp.where((khead == hh) & causal, s, NEG_INF)
        mx = jnp.max(s, axis=-1, keepdims=True)
        pr = jnp.exp2(s - mx)
        o = _dot(pr.astype(BF16), v) / jnp.sum(pr, axis=-1, keepdims=True)
        o_ref[:, cols] = o.astype(o_ref.dtype)


def _sb_sample_kernel(q_ref, kc_ref, vc_ref, kn_ref, vn_ref, o_ref, *, past, total, n_heads):
    t_rows = o_ref.shape[0]
    nb = total // LANE
    k = _join_rows(kc_ref[...], kn_ref[...], total)
    v = _join_rows(vc_ref[...], vn_ref[...], total)
    khead, causal = _interleaved_masks(t_rows, total, past, n_heads, strict=True)
    tri = _suffix_tri(LANE)
    for hh in range(n_heads):
        cols = slice(hh * LANE, (hh + 1) * LANE)
        ok = (khead == hh) & causal
        z = _dot_nt(q_ref[:, cols], k)
        log_beta = _log_sigmoid(z)
        log_fail = jnp.where(ok, log_beta - z, 0.0)
        stacked = jnp.concatenate([log_fail[:, c * LANE:(c + 1) * LANE] for c in range(nb)], axis=0)
        hi, lo = _split2(stacked)
        incl = _dot(hi, tri) + _dot(lo, tri)
        across = jnp.zeros((t_rows, 1), F32)
        w_blocks = [None] * nb
        for c in reversed(range(nb)):
            sl = slice(c * LANE, (c + 1) * LANE)
            inc = incl[c * t_rows:(c + 1) * t_rows]
            w = jnp.exp(log_beta[:, sl] + (inc - log_fail[:, sl] + across))
            w_blocks[c] = jnp.where(ok[:, sl], w, 0.0).astype(BF16)
            across = across + inc[:, 0:1]
        o_ref[:, cols] = _dot(jnp.concatenate(w_blocks, axis=-1), v).astype(o_ref.dtype)


def _sample_specs(n_streams, t_rows, past, gw):
    row = lambda b: (b, 0)
    return [pl.BlockSpec((t_rows, gw), row), pl.BlockSpec((past, gw), row), pl.BlockSpec((past, gw), row),
            pl.BlockSpec((t_rows, gw), row), pl.BlockSpec((t_rows, gw), row)]


def _pad_keys(n):
    return -(-n // LANE) * LANE


def _band_sample(q, kc, vc, kn, vn, table, n_streams):
    m, gw = q.shape
    t_rows, past = m // n_streams, kc.shape[0] // n_streams
    total = table.shape[-1]
    return pl.pallas_call(
        functools.partial(_band_sample_kernel, total=total),
        grid=(n_streams,),
        in_specs=_sample_specs(n_streams, t_rows, past, gw) + [_resident(table.shape)],
        out_specs=pl.BlockSpec((t_rows, gw), lambda b: (b, 0)),
        out_shape=jax.ShapeDtypeStruct((m, gw), BF16),
        compiler_params=_cparams("parallel"), name="band_sample",
    )(q, kc, vc, kn, vn, table)


def _interleaved_specs(t_rows, past, gw, n_heads, first_stream):
    row = lambda b: (b, 0)
    cache = pl.BlockSpec((past * n_heads, LANE), lambda b: (first_stream + b, 0))
    new = pl.BlockSpec((t_rows * n_heads, LANE), row)
    return [pl.BlockSpec((t_rows, gw), row), cache, cache, new, new]


def _fox_sample(q, kc, vc, kn, vn, f_rows, f_cols, n_streams, past, layer):
    m, gw = q.shape
    t_rows = m // n_streams
    total = f_cols.shape[-1]
    return pl.pallas_call(
        functools.partial(_fox_sample_kernel, past=past, total=total, n_heads=L_HEADS),
        grid=(n_streams,),
        in_specs=_interleaved_specs(t_rows, past, gw, L_HEADS, layer * n_streams) + [
            pl.BlockSpec((t_rows, LANE), lambda b: (b, 0)),
            pl.BlockSpec((None, 1, total), lambda b: (b, 0, 0))],
        out_specs=pl.BlockSpec((t_rows, gw), lambda b: (b, 0)),
        out_shape=jax.ShapeDtypeStruct((m, gw), BF16),
        compiler_params=_cparams("parallel"), name="fox_sample",
    )(q, kc, vc, kn, vn, f_rows, f_cols)


def _sb_sample(q, kc, vc, kn, vn, n_streams, past, layer):
    m, gw = q.shape
    t_rows = m // n_streams
    total = _pad_keys((past + t_rows) * L_HEADS)
    return pl.pallas_call(
        functools.partial(_sb_sample_kernel, past=past, total=total, n_heads=L_HEADS),
        grid=(n_streams,),
        in_specs=_interleaved_specs(t_rows, past, gw, L_HEADS, layer * n_streams),
        out_specs=pl.BlockSpec((t_rows, gw), lambda b: (b, 0)),
        out_shape=jax.ShapeDtypeStruct((m, gw), BF16),
        compiler_params=_cparams("parallel"), name="sb_sample",
    )(q, kc, vc, kn, vn)


def _mla_sample(qd, kd_c, vd_c, kd_n, vd_n, n_streams):
    n_heads, m, dk = qd.shape
    gw = vd_n.shape[1]
    t_rows, past = m // n_streams, vd_c.shape[0] // n_streams
    total = _pad_keys(past + t_rows)
    lat = lambda rows: pl.BlockSpec((n_heads, rows, dk), lambda b: (0, b, 0))
    row = lambda b: (b, 0)
    return pl.pallas_call(
        functools.partial(_mla_sample_kernel, past=past, total=total, n_heads=n_heads),
        grid=(n_streams,),
        in_specs=[lat(t_rows), lat(past), pl.BlockSpec((past, gw), row), lat(t_rows),
                  pl.BlockSpec((t_rows, gw), row)],
        out_specs=pl.BlockSpec((t_rows, gw), row),
        out_shape=jax.ShapeDtypeStruct((m, gw), BF16),
        compiler_params=_cparams("parallel"), name="mla_sample",
    )(qd, kd_c, vd_c, kd_n, vd_n)


def _out_proj_kernel(oa_ref, ob_ref, oc_ref, od_ref, gmix_ref, wo_ref, x_ref, gpost_ref, gffn_ref,
                     x1_ref, h2_ref):
    gw = oa_ref.shape[1]
    y = None
    for gi, o_ref in enumerate((oa_ref, ob_ref, oc_ref, od_ref)):
        n = _rms(o_ref[...].astype(F32), gmix_ref[gi:gi + 1, :]).astype(BF16)
        part = _dot(n, wo_ref[gi * gw:(gi + 1) * gw, :])
        y = part if y is None else y + part
    x1 = x_ref[...] + _rms(y, gpost_ref[...])
    x1_ref[...] = x1
    h2_ref[...] = _rms(x1, gffn_ref[...]).astype(BF16)


def _out_proj(outs, g_mix, w_o, x, g_post, g_ffn, tm):
    m, d = x.shape
    gw = d // 4
    rows = lambda i: (i, 0)
    return pl.pallas_call(
        _out_proj_kernel, grid=(m // tm,),
        in_specs=[pl.BlockSpec((tm, gw), rows)] * 4 + [_resident(g_mix.shape), _resident(w_o.shape),
                                                       pl.BlockSpec((tm, d), rows),
                                                       _resident((1, d)), _resident((1, d))],
        out_specs=[pl.BlockSpec((tm, d), rows), pl.BlockSpec((tm, d), rows)],
        out_shape=[jax.ShapeDtypeStruct((m, d), F32), jax.ShapeDtypeStruct((m, d), BF16)],
        compiler_params=_cparams("parallel"), name="out_proj",
    )(*outs, g_mix, w_o, x, g_post, g_ffn)


FFN_ROW_GROUP = 256


def _ffn_kernel(h_ref, wg_ref, wu_ref, wd_ref, x_ref, g_ref, o_ref, acc_ref):
    f = pl.program_id(1)

    @pl.when(f == 0)
    def _():
        acc_ref[...] = jnp.zeros_like(acc_ref)

    tm = h_ref.shape[0]
    rg = min(FFN_ROW_GROUP, tm)

    def gate_up(r):
        h = h_ref[r * rg:(r + 1) * rg, :]
        return _dot(h, wg_ref[...]), _dot(h, wu_ref[...])

    pending = [gate_up(0)]
    for r in range(tm // rg):
        if r + 1 < tm // rg:
            pending.append(gate_up(r + 1))
        gate, up = pending.pop(0)
        act = (gate * jax.nn.sigmoid(gate) * up).astype(BF16)
        acc_ref[r * rg:(r + 1) * rg, :] += _dot(act, wd_ref[...])

    @pl.when(f == pl.num_programs(1) - 1)
    def _():
        o_ref[...] = x_ref[...] + _rms(acc_ref[...], g_ref[...])


def _ffn(h2, w_gate, w_up, w_down, x1, g, tm, tf):
    m, d = x1.shape
    ff = w_gate.shape[1]
    rows = lambda i, f: (i, 0)
    return pl.pallas_call(
        _ffn_kernel, grid=(m // tm, ff // tf),
        in_specs=[pl.BlockSpec((tm, d), rows), pl.BlockSpec((d, tf), lambda i, f: (0, f)),
                  pl.BlockSpec((d, tf), lambda i, f: (0, f)), pl.BlockSpec((tf, d), lambda i, f: (f, 0)),
                  pl.BlockSpec((tm, d), rows), pl.BlockSpec((1, d), lambda i, f: (0, 0))],
        out_specs=pl.BlockSpec((tm, d), rows),
        out_shape=jax.ShapeDtypeStruct((m, d), F32),
        scratch_shapes=[pltpu.VMEM((tm, d), F32)],
        compiler_params=_cparams("parallel", "arbitrary"), name="ffn",
    )(h2, w_gate, w_up, w_down, x1, g)


def _pad_cols(w, n):
    return jnp.pad(w, ((0, 0), (0, n - w.shape[1])))


def _swap_halves(w):
    half = w.shape[1] // 2
    return jnp.concatenate([w[:, half:], w[:, :half]], axis=1)


def _layer_weights(l, d, w_in, b_f, g_q_lat, g_kv_lat, w_uq, w_uk, w_uv):
    gw = d // 4
    q_rank, kv_rank = g_q_lat.shape[1], g_kv_lat.shape[1]
    n_ab = 6 * gw
    n_in = w_in.shape[2]
    w_t = jnp.swapaxes(w_in[l], 0, 1)
    w_ab = w_t[:n_ab].astype(BF16)
    w_rest = jnp.pad(w_t[n_ab:], ((0, -(-(n_in - n_ab) // LANE) * LANE - (n_in - n_ab)), (0, 0))).astype(BF16)
    hd = NOPE_DIM + ROPE_DIM
    uq = w_uq[l]
    nope = [uq[:, h * hd:h * hd + NOPE_DIM] for h in range(L_HEADS)]
    rope = [uq[:, h * hd + NOPE_DIM:(h + 1) * hd] for h in range(L_HEADS)]
    w_q = jnp.concatenate(nope + [_pad_cols(r, LANE) for r in rope], axis=1).astype(BF16)
    w_qs = jnp.concatenate([_pad_cols(_swap_halves(r), LANE) for r in rope], axis=1).astype(BF16)
    return dict(w_ab=w_ab, w_rest=w_rest, b_f=_pad_cols(b_f[l][None, :].astype(F32), LANE), g_q=g_q_lat[l][None, :],
                g_kv=g_kv_lat[l][None, :], w_q=w_q, w_qs=w_qs,
                w_uk=w_uk[l].astype(BF16), w_uv=w_uv[l].astype(BF16))


def _rope_tables(pos):
    half = ROPE_DIM // 2
    inv_freq = ROPE_THETA ** (-jnp.arange(half, dtype=F32) / half)
    ang = pos.astype(F32)[:, None] * inv_freq[None, :]
    cos, sin = jnp.cos(ang), jnp.sin(ang)
    zero = jnp.zeros((pos.shape[0], LANE - ROPE_DIM), F32)
    return jnp.concatenate([cos, cos, zero], axis=1), jnp.concatenate([-sin, sin, zero], axis=1)


def kernel(x_prompt, x_sample, cache_a_k, cache_a_v, cache_b_k, cache_b_v, cache_b_logf, cache_c_k, cache_c_v,
           cache_d_ckv, cache_d_kpe, w_in, b_f, rel_bias, g_q_lat, g_kv_lat, w_uq, w_uk, w_uv, g_mix, w_o,
           g_pre_attn, g_post_attn, g_pre_ffn, g_post_ffn, w_gate, w_up, w_down):
    batch, seq, d = x_prompt.shape
    n_streams, t_rows, _ = x_sample.shape
    depth = w_in.shape[0]
    past = cache_b_k.shape[2]
    a_win = cache_a_k.shape[2]
    gw = d // 4
    mp, ms = batch * seq, n_streams * t_rows
    bq, rb = 512, 128
    tm_p, tm_s = 256, ms

    cos_p, sin_p = _rope_tables(jnp.arange(seq))
    cos_s, sin_s = _rope_tables(jnp.tile(past + jnp.arange(t_rows), n_streams))

    cache_ckv = cache_d_ckv.reshape(depth * n_streams * past, -1)
    cache_kpe128 = _pad_cols(cache_d_kpe.reshape(depth * n_streams * past, ROPE_DIM), LANE)
    xp = x_prompt.reshape(mp, d)
    xs = x_sample.reshape(ms, d)
    p_states, s_states = [], []
    stacked = None
    for l in range(depth):
        lw = _layer_weights(l, d, w_in, b_f, g_q_lat, g_kv_lat, w_uq, w_uk, w_uv)
        g_attn = g_pre_attn[l][None, :]
        wo = w_o[l].astype(BF16)
        wg, wu, wd = w_gate[l].astype(BF16), w_up[l].astype(BF16), w_down[l].astype(BF16)
        g_post, g_ffn, g_out = g_post_attn[l][None, :], g_pre_ffn[l][None, :], g_post_ffn[l][None, :]

        pr = _project(xp, g_attn, lw, cos_p, sin_p, tm_p, stack=(l, depth, stacked))
        stacked = {name: pr[name] for name in _STACKED_STATES}
        f_run = _cumsum(pr['logf'], seq)
        f_cols = (-f_run[:, :L_HEADS]).reshape(batch, seq, L_HEADS).transpose(0, 2, 1)
        f_cols = f_cols.reshape(batch, L_HEADS, seq // bq, bq)
        bq_flash = 2 * bq if seq % (2 * bq) == 0 else bq
        oa = _band_prompt(pr['qa'], pr['ka16'], pr['va16'], _band_table(rel_bias[l], rb), batch, seq, bq, rb)
        ob = _fox_prompt(pr['qb'], pr['kb16'], pr['vb16'], f_run, f_cols, batch, seq, bq_flash, bq, 2 * rb, 2)
        oc = _sb_prompt(pr['qc'], pr['kc16'], pr['vc16'], batch, seq, bq, rb)
        od = _mla_prompt(pr['qd'], pr['kd'], pr['vd'], batch, seq, bq_flash, bq, 4 * rb, 1)
        x1, h2 = _out_proj((oa, ob, oc, od), g_mix[l], wo, xp, g_post, g_ffn, tm_p)
        xp = _ffn(h2, wg, wu, wd, x1, g_out, 512, 512)
        keep = min(a_win, seq)
        p_states.append((
            pr['ka'].reshape(batch, seq, gw)[:, seq - keep:].reshape(batch, keep, A_HEADS, gw // A_HEADS),
            pr['va'].reshape(batch, seq, gw)[:, seq - keep:].reshape(batch, keep, A_HEADS, gw // A_HEADS),
            pr['logf'][:, :L_HEADS].reshape(batch, seq, L_HEADS),
            pr['ckv'].reshape(batch, seq, -1), pr['kpe'].reshape(batch, seq, ROPE_DIM)))

        sr = _project(xs, g_attn, lw, cos_s, sin_s, tm_s)
        qa, ka, va, qb, kb, vb, qc, kc, vc = (sr[n] for n in ('qa', 'ka', 'va', 'qb', 'kb', 'vb', 'qc', 'kc', 'vc'))
        logf, qd, ckv, kpe = (sr[n] for n in ('logf', 'qd', 'ckv', 'kpe'))
        tot_a = _pad_keys(a_win + t_rows)
        rel = (a_win + np.arange(t_rows))[:, None] - np.arange(tot_a)[None, :]
        tab = rel_bias[l][:, np.clip(rel, -REL_CLIP, REL_CLIP) + REL_CLIP].astype(F32)
        tab = jnp.where(jnp.asarray(np.arange(tot_a) < a_win + t_rows)[None, None, :], tab, NEG_INF)
        oa = _band_sample(qa, cache_a_k[l].reshape(n_streams * a_win, gw),
                          cache_a_v[l].reshape(n_streams * a_win, gw), ka, va, tab, n_streams)
        f_new = _cumsum(logf, t_rows)
        cl = _pad_cols(cache_b_logf[l].reshape(n_streams * past, L_HEADS).astype(F32), LANE)
        suffix = _cumsum(cl, past, reverse=True)
        n_keys = (past + t_rows) * L_HEADS
        f_cols = jnp.concatenate([suffix[:, :L_HEADS].reshape(n_streams, past * L_HEADS),
                                  -f_new[:, :L_HEADS].reshape(n_streams, t_rows * L_HEADS),
                                  jnp.zeros((n_streams, _pad_keys(n_keys) - n_keys), F32)], axis=1)[:, None, :]
        rows_of = lambda a: a.reshape(-1, LANE)
        ob = _fox_sample(qb, rows_of(cache_b_k), rows_of(cache_b_v), rows_of(kb), rows_of(vb),
                         f_new, f_cols, n_streams, past, l)
        oc = _sb_sample(qc, rows_of(cache_c_k), rows_of(cache_c_v), rows_of(kc), rows_of(vc), n_streams, past, l)
        kd_c, vd_c = _mla_keys(cache_ckv, cache_kpe128, lw['w_uk'], lw['w_uv'], 512,
                               m=n_streams * past, first_block=l * n_streams * past // 512)
        od = _mla_sample(qd, kd_c, vd_c, sr['kd'], sr['vd'], n_streams)
        x1, h2 = _out_proj((oa, ob, oc, od), g_mix[l], wo, xs, g_post, g_ffn, tm_s)
        xs = _ffn(h2, wg, wu, wd, x1, g_out, ms, 512)
        s_states.append((
            ka.reshape(n_streams, t_rows, A_HEADS, gw // A_HEADS),
            va.reshape(n_streams, t_rows, A_HEADS, gw // A_HEADS),
            kb.reshape(n_streams, t_rows, L_HEADS, LANE), vb.reshape(n_streams, t_rows, L_HEADS, LANE),
            logf[:, :L_HEADS].reshape(n_streams, t_rows, L_HEADS),
            kc.reshape(n_streams, t_rows, L_HEADS, LANE), vc.reshape(n_streams, t_rows, L_HEADS, LANE),
            ckv.reshape(n_streams, t_rows, -1), kpe.reshape(n_streams, t_rows, ROPE_DIM)))

    p_a_k, p_a_v, p_b_logf, p_d_ckv, p_d_kpe = [jnp.stack(s) for s in zip(*p_states)]
    p_b_k, p_b_v, p_c_k, p_c_v = (stacked[name].reshape(depth, batch, seq, L_HEADS, LANE)
                                  for name in _STACKED_STATES)
    s_out = [jnp.stack(s) for s in zip(*s_states)]
    return (xp.reshape(batch, seq, d), xs.reshape(n_streams, t_rows, d),
            p_a_k, p_a_v, p_b_k, p_b_v, p_b_logf, p_c_k, p_c_v, p_d_ckv, p_d_kpe, *s_out)
```

```python
import functools
import math

import numpy as np
import jax
import jax.numpy as jnp
from jax import lax
from jax.experimental import pallas as pl
from jax.experimental.pallas import tpu as pltpu

CHUNK = 64
A_HEADS = 8
L_HEADS = 4
BAND_CHUNKS = 8
REL_CLIP = 128
NOPE_DIM = 128
ROPE_DIM = 64
ROPE_THETA = 10000.0
RMS_EPS = 1e-6
NEG_INF = -1e30
LOG2E = math.log2(math.e)

LANE = 128
MXU_DEPTH = 256
VMEM_LIMIT = 56 * 1024 * 1024

F32 = jnp.float32
BF16 = jnp.bfloat16


def _cparams(*sem):
    return pltpu.CompilerParams(dimension_semantics=sem, vmem_limit_bytes=VMEM_LIMIT)


def _resident(shape):
    return pl.BlockSpec(shape, lambda *_: (0,) * len(shape), pipeline_mode=pl.Buffered(1))


def _rms(x, g):
    return x * lax.rsqrt(jnp.mean(x * x, axis=-1, keepdims=True) + RMS_EPS) * g


def _log_sigmoid(x):
    return jnp.minimum(x, 0.0) - jnp.log1p(jnp.exp(-jnp.abs(x)))


def _dot(a, b):
    return jnp.dot(a, b, preferred_element_type=F32)


def _dot_nt(a, b):
    return lax.dot_general(a, b, (((1,), (1,)), ((), ())), preferred_element_type=F32)


def _split2(x):
    hi = x.astype(BF16)
    lo = (x - hi.astype(F32)).astype(BF16)
    return hi, lo


def _split3(x):
    hi = x.astype(BF16)
    r = x - hi.astype(F32)
    mid = r.astype(BF16)
    lo = (r - mid.astype(F32)).astype(BF16)
    return hi, mid, lo


_PROJ_INPUTS = 13


def _latent_keys(ckv, kpe, wuk_ref, wuv_ref, kd_ref, vd_ref):
    c = ckv.astype(BF16)
    kn = _dot(c, wuk_ref[...])
    vd_ref[...] = _dot(c, wuv_ref[...]).astype(BF16)
    kp = kpe.astype(BF16)
    for hh in range(kd_ref.shape[0]):
        kd_ref[hh, :, 0:LANE] = kn[:, LANE * hh:LANE * (hh + 1)].astype(BF16)
        kd_ref[hh, :, LANE:2 * LANE] = kp


def _proj_kernel(*refs, gw, q_rank, kv_rank, rope_dim, q_scales):
    (x_ref, g_ref, wab_ref, wrest_ref, bf_ref, gq_ref, gkv_ref, wq_ref, wqs_ref, wuk_ref, wuv_ref, cos_ref, sin_ref,
     ) = refs[:_PROJ_INPUTS]
    (qa_ref, ka_ref, va_ref, ka16_ref, va16_ref, qb_ref, kb_ref, vb_ref, kb16_ref, vb16_ref,
     qc_ref, kc_ref, vc_ref, kc16_ref, vc16_ref, logf_ref, qd_ref, ckv_ref, kpe_ref, kd_ref, vd_ref,
     ) = refs[len(refs) - len(_PROJ_OUTPUTS):]
    h = _rms(x_ref[...], g_ref[...]).astype(BF16)

    n_ab = wab_ref.shape[0]
    z_rest = _dot_nt(h, wrest_ref[...])
    zf = z_rest[:, 0:LANE]
    z_rest = pltpu.roll(z_rest, z_rest.shape[1] - L_HEADS, 1)

    def seg(a, n):
        if a < n_ab:
            return _dot_nt(h, wab_ref[a:a + n, :])
        return z_rest[:, a - n_ab:a - n_ab + n]

    groups = ((qa_ref, ka_ref, va_ref, ka16_ref, va16_ref), (qb_ref, kb_ref, vb_ref, kb16_ref, vb16_ref),
              (qc_ref, kc_ref, vc_ref, kc16_ref, vc16_ref))
    for gi, (q_r, k_r, v_r, k16_r, v16_r) in enumerate(groups):
        base = 3 * gw * gi
        q_r[...] = (seg(base, gw) * q_scales[gi]).astype(BF16)
        k = seg(base + gw, gw)
        k16_r[...] = k.astype(BF16)
        v = seg(base + 2 * gw, gw)
        v16_r[...] = v.astype(BF16)
        for state_ref, val in ((k_r, k), (v_r, v)):
            if state_ref.shape[-1] == gw:
                state_ref[...] = val
            else:
                n_h = gw // LANE
                for hh in range(n_h):
                    state_ref[pl.ds(hh, val.shape[0], stride=n_h), :] = val[:, hh * LANE:(hh + 1) * LANE]
    off = 9 * gw
    dq = seg(off, q_rank)
    off += q_rank
    dkv = seg(off, kv_rank)
    off += kv_rank
    dkr = seg(off, LANE)
    half = rope_dim // 2
    lane = lax.broadcasted_iota(jnp.int32, dkr.shape, 1)
    dkr_sw = jnp.where(lane < half, pltpu.roll(dkr, LANE - half, 1), pltpu.roll(dkr, half, 1))

    logf_ref[...] = _log_sigmoid(zf + bf_ref[...])
    ckv = _rms(dkv, gkv_ref[...])
    ckv_ref[...] = ckv
    cos = cos_ref[...]
    sin = sin_ref[...]
    kpe = dkr * cos + dkr_sw * sin
    kpe_ref[...] = kpe[:, :rope_dim]
    _latent_keys(ckv, kpe, wuk_ref, wuv_ref, kd_ref, vd_ref)

    nq = _rms(dq, gq_ref[...]).astype(BF16)
    qf = _dot(nq, wq_ref[...])
    qs = _dot(nq, wqs_ref[...])
    n_heads = qd_ref.shape[0]
    for hh in range(n_heads):
        qd_ref[hh, :, 0:LANE] = (qf[:, LANE * hh:LANE * (hh + 1)] * q_scales[3]).astype(BF16)
        r0 = LANE * (n_heads + hh)
        rot = qf[:, r0:r0 + LANE] * cos + qs[:, LANE * hh:LANE * (hh + 1)] * sin
        qd_ref[hh, :, LANE:2 * LANE] = (rot * q_scales[3]).astype(BF16)


def _q_scales(gw):
    return ((gw // A_HEADS) ** -0.5 * LOG2E, (gw // L_HEADS) ** -0.5 * LOG2E, (gw // L_HEADS) ** -0.5,
            (NOPE_DIM + ROPE_DIM) ** -0.5 * LOG2E)


_PROJ_OUTPUTS = ('qa', 'ka', 'va', 'ka16', 'va16', 'qb', 'kb', 'vb', 'kb16', 'vb16',
                 'qc', 'kc', 'vc', 'kc16', 'vc16', 'logf', 'qd', 'ckv', 'kpe', 'kd', 'vd')


_STACKED_STATES = ('kb', 'vb', 'kc', 'vc')


def _project(x, g, lw, cos_t, sin_t, tm, stack=None):
    m, d = x.shape
    gw = d // 4
    q_rank, kv_rank = lw['g_q'].shape[1], lw['g_kv'].shape[1]
    n_pos_blocks = cos_t.shape[0] // tm
    rows = lambda i: (i, 0)
    pos = lambda i: (i % n_pos_blocks, 0)
    f32o = lambda n: jax.ShapeDtypeStruct((m, n), F32)
    bf16o = lambda n: jax.ShapeDtypeStruct((m, n), BF16)
    blk = lambda n: pl.BlockSpec((tm, n), rows)
    group = [bf16o(gw), f32o(gw), f32o(gw), bf16o(gw), bf16o(gw)]
    per_head = jax.ShapeDtypeStruct((L_HEADS, m, 2 * LANE), BF16)
    per_head_blk = pl.BlockSpec((L_HEADS, tm, 2 * LANE), lambda i: (0, i, 0))
    out_shape = group * 3 + [f32o(LANE), per_head, f32o(kv_rank), f32o(ROPE_DIM), per_head, bf16o(gw)]
    out_specs = [blk(gw)] * 15 + [blk(LANE), per_head_blk, blk(kv_rank), blk(ROPE_DIM), per_head_blk, blk(gw)]
    in_specs = [blk(d), _resident((1, d)), _resident(lw['w_ab'].shape), _resident(lw['w_rest'].shape),
                _resident((1, LANE)),
                _resident((1, q_rank)), _resident((1, kv_rank)),
                _resident(lw['w_q'].shape), _resident(lw['w_qs'].shape),
                _resident(lw['w_uk'].shape), _resident(lw['w_uv'].shape),
                pl.BlockSpec((tm, LANE), pos), pl.BlockSpec((tm, LANE), pos)]
    args = [x, g, lw['w_ab'], lw['w_rest'], lw['b_f'], lw['g_q'], lw['g_kv'], lw['w_q'], lw['w_qs'],
            lw['w_uk'], lw['w_uv'], cos_t, sin_t]
    aliases = {}
    if stack is not None:
        layer, depth, buffers = stack
        for name in _STACKED_STATES:
            o = _PROJ_OUTPUTS.index(name)
            n_h = gw // LANE
            out_shape[o] = jax.ShapeDtypeStruct((depth, m * n_h, LANE), F32)
            out_specs[o] = pl.BlockSpec((None, tm * n_h, LANE), lambda i: (layer, i, 0))
            if buffers is not None:
                aliases[len(args)] = o
                args.append(buffers[name])
                in_specs.append(pl.BlockSpec(memory_space=pl.ANY))
    kern = functools.partial(_proj_kernel, gw=gw, q_rank=q_rank, kv_rank=kv_rank, rope_dim=ROPE_DIM,
                             q_scales=_q_scales(gw))
    outs = pl.pallas_call(
        kern, grid=(m // tm,), in_specs=in_specs, out_specs=out_specs, out_shape=out_shape,
        input_output_aliases=aliases, compiler_params=_cparams("parallel"), name="in_proj",
    )(*args)
    return dict(zip(_PROJ_OUTPUTS, outs))


def _mla_keys_kernel(ckv_ref, kpe_ref, wuk_ref, wuv_ref, kd_ref, vd_ref):
    _latent_keys(ckv_ref[...], kpe_ref[...], wuk_ref, wuv_ref, kd_ref, vd_ref)


def _mla_keys(ckv, kpe128, w_uk, w_uv, tm, m=None, first_block=0):
    r = ckv.shape[1]
    m = ckv.shape[0] if m is None else m
    n = w_uk.shape[1]
    rows = lambda i: (i, 0)
    src = lambda i: (first_block + i, 0)
    return pl.pallas_call(
        _mla_keys_kernel, grid=(m // tm,),
        in_specs=[pl.BlockSpec((tm, r), src), pl.BlockSpec((tm, LANE), src),
                  _resident(w_uk.shape), _resident(w_uv.shape)],
        out_specs=[pl.BlockSpec((L_HEADS, tm, 2 * LANE), lambda i: (0, i, 0)), pl.BlockSpec((tm, n), rows)],
        out_shape=[jax.ShapeDtypeStruct((L_HEADS, m, 2 * LANE), BF16), jax.ShapeDtypeStruct((m, n), BF16)],
        compiler_params=_cparams("parallel"), name="mla_keys",
    )(ckv, kpe128, w_uk, w_uv)


def _cumsum_kernel(x_ref, o_ref, carry_ref, *, seg, reverse):
    @pl.when(pl.program_id(1) == 0)
    def _():
        carry_ref[...] = jnp.zeros_like(carry_ref)

    x = x_ref[...]
    tc = x.shape[0]
    r = lax.broadcasted_iota(jnp.int32, (tc, tc), 0)
    c = lax.broadcasted_iota(jnp.int32, (tc, tc), 1)
    keep = (c > r) if reverse else (c <= r)
    if seg < tc:
        keep = keep & (r // seg == c // seg)
    tri = jnp.where(keep, 1.0, 0.0).astype(BF16)
    hi, mid, lo = _split3(x)
    y = _dot(tri, hi) + _dot(tri, mid) + _dot(tri, lo)
    o_ref[...] = y + carry_ref[...]
    if seg > tc:
        carry_ref[...] += jnp.sum(x, axis=0, keepdims=True)


def _cumsum(x, seg, reverse=False):
    m, n = x.shape
    tc = min(512, m) if seg >= 512 else min(256, m)
    nb = max(seg // tc, 1)
    nseg = m // (nb * tc)
    if reverse:
        idx = lambda s, j: (s * nb + nb - 1 - j, 0)
    else:
        idx = lambda s, j: (s * nb + j, 0)
    return pl.pallas_call(
        functools.partial(_cumsum_kernel, seg=seg, reverse=reverse), grid=(nseg, nb),
        in_specs=[pl.BlockSpec((tc, n), idx)], out_specs=pl.BlockSpec((tc, n), idx),
        out_shape=jax.ShapeDtypeStruct((m, n), F32),
        scratch_shapes=[pltpu.VMEM((1, n), F32)],
        compiler_params=_cparams("parallel", "arbitrary"), name="cumsum",
    )(x)


def _lane_tile(x, n):
    return x if n == LANE else jnp.concatenate([x] * (n // LANE), axis=1)


FOX_UNDERFLOW = 152.0


def _flash_kernel(*refs, fox, bq, bk, rb, ahead):
    if fox:
        q_ref, k_ref, v_ref, fq_ref, fk_ref, o_ref, acc_ref, m_ref, l_ref, kmax_ref = refs
    else:
        q_ref, k_ref, v_ref, o_ref, acc_ref, m_ref, l_ref = refs
    head = pl.program_id(1)
    qi = pl.program_id(2)
    m_ref[...] = jnp.full_like(m_ref, NEG_INF)
    l_ref[...] = jnp.zeros_like(l_ref)
    acc_ref[...] = jnp.zeros_like(acc_ref)
    if fox:
        lane = lax.broadcasted_iota(jnp.int32, fq_ref.shape, 1)
        fq = jnp.sum(jnp.where(lane == head, fq_ref[...], 0.0), axis=-1, keepdims=True) * LOG2E

    def logits(r, j):
        rows = slice(r * rb, (r + 1) * rb)
        s = _dot_nt(q_ref[rows, :], k_ref[pl.ds(pl.multiple_of(j * bk, bk), bk), :])
        if fox:
            s = fq[rows] + (fk_ref[pl.ds(j, 1), :] * LOG2E + s)
        return s

    def update(r, s, j):
        rows = slice(r * rb, (r + 1) * rb)
        m_prev = m_ref[rows, :]
        m_new = jnp.maximum(m_prev, jnp.max(s, axis=-1, keepdims=True))
        alpha = jnp.exp2(m_prev - m_new)
        p = jnp.exp2(s - _lane_tile(m_new, s.shape[1]))
        l_ref[rows, :] = alpha * l_ref[rows, :] + jnp.sum(p, axis=-1, keepdims=True)
        v = v_ref[pl.ds(pl.multiple_of(j * bk, bk), bk), :]
        acc_ref[rows, :] = alpha * acc_ref[rows, :] + _dot(p.astype(BF16), v)
        m_ref[rows, :] = m_new

    def run(items):
        pending = [logits(r, j) for r, j, _ in items[:ahead]]
        for i, (r, j, diag) in enumerate(items):
            if i + ahead < len(items):
                pending.append(logits(*items[i + ahead][:2]))
            s = pending.pop(0)
            if diag is not None:
                row = r * rb + lax.broadcasted_iota(jnp.int32, s.shape, 0)
                col = diag + lax.broadcasted_iota(jnp.int32, s.shape, 1)
                ok = (col <= row) if fox else (col // CHUNK <= row // CHUNK)
                s = jnp.where(ok, s, NEG_INF)
            update(r, s, j)

    n_groups = bq // rb
    kb_per_q = bq // bk

    def full_block(j, carry):
        run([(r, j, None) for r in range(n_groups)])
        return carry

    items = []
    for kb in range(kb_per_q):
        first = kb * bk // rb
        items += [(r, qi * kb_per_q + kb, kb * bk) for r in range(first, first + bk // rb)]
        items += [(r, qi * kb_per_q + kb, None) for r in range(first + bk // rb, n_groups)]
    n_full = qi * kb_per_q
    if not fox:
        lax.fori_loop(0, n_full, full_block, 0)
        run(items)
    else:
        @pl.when(qi == 0)
        def _():
            def norm_max(c, best):
                kc = k_ref[pl.ds(pl.multiple_of(c * bk, bk), bk), :].astype(F32)
                return jnp.maximum(best, jnp.max(jnp.sum(kc * kc, axis=-1, keepdims=True)))
            kmax_ref[0] = jnp.sqrt(lax.fori_loop(0, k_ref.shape[0] // bk, norm_max, jnp.float32(0.0)))

        run(items)
        q32 = q_ref[...].astype(F32)
        row_bound = jnp.sqrt(jnp.sum(q32 * q32, axis=-1, keepdims=True)) * (kmax_ref[0] * 1.001) + fq + 1.0

        margin = jnp.max(row_bound - m_ref[:, 0:1])
        last_key = fk_ref[...][:, bk - 1:bk] * LOG2E
        block = lax.broadcasted_iota(jnp.int32, last_key.shape, 0)
        matters = (margin + last_key >= -FOX_UNDERFLOW) & (block < n_full)
        oldest = jnp.min(jnp.where(matters, block, n_full))

        def older_block(t, carry):
            return full_block(n_full - 1 - t, carry)

        lax.fori_loop(0, n_full - oldest, older_block, 0)
    o_ref[...] = (acc_ref[...] / l_ref[...]).astype(o_ref.dtype)


def _flash_scratch(bq):
    return [pltpu.VMEM((bq, LANE), F32), pltpu.VMEM((bq, LANE), F32), pltpu.VMEM((bq, LANE), F32)]


def _fox_prompt(q, k, v, f_rows, f_cols, batch, seq, bq, bk, rb, ahead):
    m, gw = q.shape
    nq = seq // bq
    qmap = lambda b, h, i: (b * nq + i, h)
    kvmap = lambda b, h, i: (b, h)
    return pl.pallas_call(
        functools.partial(_flash_kernel, fox=True, bq=bq, bk=bk, rb=rb, ahead=ahead),
        grid=(batch, L_HEADS, nq),
        in_specs=[pl.BlockSpec((bq, LANE), qmap), pl.BlockSpec((seq, LANE), kvmap),
                  pl.BlockSpec((seq, LANE), kvmap),
                  pl.BlockSpec((bq, LANE), lambda b, h, i: (b * nq + i, 0)),
                  pl.BlockSpec((None, None, seq // bk, bk), lambda b, h, i: (b, h, 0, 0))],
        out_specs=pl.BlockSpec((bq, LANE), qmap),
        out_shape=jax.ShapeDtypeStruct((m, gw), BF16),
        scratch_shapes=_flash_scratch(bq) + [pltpu.SMEM((1,), F32)],
        compiler_params=_cparams("parallel", "parallel", "arbitrary"), name="fox_prompt",
    )(q, k, v, f_rows, f_cols)


def _mla_prompt(qd, kd, vd, batch, seq, bq, bk, rb, ahead):
    n_heads, m, dk = qd.shape
    nq = seq // bq
    return pl.pallas_call(
        functools.partial(_flash_kernel, fox=False, bq=bq, bk=bk, rb=rb, ahead=ahead),
        grid=(batch, n_heads, nq),
        in_specs=[pl.BlockSpec((None, bq, dk), lambda b, h, i: (h, b * nq + i, 0)),
                  pl.BlockSpec((None, seq, dk), lambda b, h, i: (h, b, 0)),
                  pl.BlockSpec((seq, LANE), lambda b, h, i: (b, h))],
        out_specs=pl.BlockSpec((bq, LANE), lambda b, h, i: (b * nq + i, h)),
        out_shape=jax.ShapeDtypeStruct((m, n_heads * LANE), BF16),
        scratch_shapes=_flash_scratch(bq),
        compiler_params=_cparams("parallel", "parallel", "arbitrary"), name="mla_prompt",
    )(qd, kd, vd)


SB_AHEAD = 4
SB_UNDERFLOW = 105.0


def _suffix_tri(n):
    r = lax.broadcasted_iota(jnp.int32, (n, n), 0)
    c = lax.broadcasted_iota(jnp.int32, (n, n), 1)
    return jnp.where(r >= c, 1.0, 0.0).astype(BF16)


def _sb_kernel(q_ref, k_ref, v_ref, o_ref, acc_ref, across_ref, *, bq, rb, sub):
    qi = pl.program_id(2)
    acc_ref[...] = jnp.zeros_like(acc_ref)
    across_ref[...] = jnp.zeros_like(across_ref)
    tri = _suffix_tri(sub)

    n_groups = bq // rb

    def visit(start, diagonal):
        n_keys = [-(-((r + 1) * rb) // sub) * sub if diagonal else sub for r in range(n_groups)]

        def stage_logits(r):
            rows = slice(r * rb, (r + 1) * rb)
            return _dot_nt(q_ref[rows, :], k_ref[pl.ds(start, n_keys[r]), :])

        def stage_sums(r, z):
            log_fail = -jnp.maximum(z, 0.0) - jnp.log(1.0 + jnp.exp(-jnp.abs(z)))
            log_beta = log_fail + z
            ok = None
            if diagonal:
                row = r * rb + lax.broadcasted_iota(jnp.int32, z.shape, 0)
                col = lax.broadcasted_iota(jnp.int32, z.shape, 1)
                ok = col < row
                log_fail = jnp.where(ok, log_fail, 0.0)
            incl = {}
            for c in reversed(range(n_keys[r] // sub)):
                hi, lo = _split2(log_fail[:, c * sub:(c + 1) * sub])
                incl[c] = _dot(hi, tri) + _dot(lo, tri)
            return log_fail, log_beta, ok, incl

        def stage_values(r, state):
            log_fail, log_beta, ok, incl = state
            rows = slice(r * rb, (r + 1) * rb)
            across = across_ref[rows, :]
            acc = acc_ref[rows, :]
            for c in reversed(range(n_keys[r] // sub)):
                sl = slice(c * sub, (c + 1) * sub)
                w = jnp.exp(log_beta[:, sl] + (incl[c] - log_fail[:, sl] + _lane_tile(across, sub)))
                if diagonal:
                    w = jnp.where(ok[:, sl], w, 0.0)
                vb = v_ref[pl.ds(pl.multiple_of(start + c * sub, sub), sub), :]
                acc = acc + _dot(w.astype(BF16), vb)
                across = across + jnp.broadcast_to(incl[c][:, 0:1], across.shape)
            across_ref[rows, :] = across
            acc_ref[rows, :] = acc

        zs = [stage_logits(r) for r in range(n_groups)]
        states = [stage_sums(r, zs[r]) for r in range(min(SB_AHEAD, n_groups))]
        for r in range(n_groups):
            stage_values(r, states[r])
            if r + SB_AHEAD < n_groups:
                states.append(stage_sums(r + SB_AHEAD, zs[r + SB_AHEAD]))

    visit(pl.multiple_of(qi * bq, bq), True)

    def more(carry):
        t, worst = carry
        return jnp.logical_and(t < qi * (bq // sub), worst > -SB_UNDERFLOW)

    def earlier_block(carry):
        t, _ = carry
        visit(pl.multiple_of(qi * bq - (t + 1) * sub, sub), False)
        return t + 1, jnp.max(across_ref[...])

    lax.while_loop(more, earlier_block, (jnp.int32(0), jnp.max(across_ref[...])))
    o_ref[...] = acc_ref[...].astype(o_ref.dtype)


def _sb_prompt(q, k, v, batch, seq, bq, rb):
    m, gw = q.shape
    nq = seq // bq
    qmap = lambda b, h, i: (b * nq + i, h)
    kvmap = lambda b, h, i: (b, h)
    return pl.pallas_call(
        functools.partial(_sb_kernel, bq=bq, rb=rb, sub=MXU_DEPTH),
        grid=(batch, L_HEADS, nq),
        in_specs=[pl.BlockSpec((bq, LANE), qmap), pl.BlockSpec((seq, LANE), kvmap),
                  pl.BlockSpec((seq, LANE), kvmap)],
        out_specs=pl.BlockSpec((bq, LANE), qmap),
        out_shape=jax.ShapeDtypeStruct((m, gw), BF16),
        scratch_shapes=[pltpu.VMEM((bq, LANE), F32), pltpu.VMEM((bq, LANE), F32)],
        compiler_params=_cparams("parallel", "parallel", "arbitrary"), name="sb_prompt",
    )(q, k, v)


BAND_AHEAD = 4


def _band_kernel(q_ref, kp_ref, kc_ref, vp_ref, vc_ref, t_ref, o_ref, *, bq, rb):
    band = BAND_CHUNKS * CHUNK
    has_prev = pl.program_id(2) > 0
    first_half = lax.broadcasted_iota(jnp.int32, (rb, LANE), 1) < (LANE // 2)

    def band_cols(r):
        lo = bq - band + r * rb
        n_cur = (r + 1) * rb
        return lo, n_cur

    def stage_logits(r, hh):
        rows = slice(r * rb, (r + 1) * rb)
        lo, n_cur = band_cols(r)
        q = q_ref[rows, :]
        sel = first_half if hh == 0 else jnp.logical_not(first_half)
        qh = jnp.where(sel, q, jnp.zeros_like(q))
        return _dot_nt(qh, kc_ref[0:n_cur, :]), _dot_nt(qh, kp_ref[lo:bq, :])

    def stage_softmax(r, hh, raw):
        lo, n_cur = band_cols(r)
        n_prev = bq - lo
        sc = t_ref[hh, :, n_prev:n_prev + n_cur] + raw[0]
        sp = t_ref[hh, :, 0:n_prev] + raw[1]
        sp = jnp.where(has_prev, sp, NEG_INF)
        mx = jnp.maximum(jnp.max(sc, axis=-1, keepdims=True), jnp.max(sp, axis=-1, keepdims=True))
        pc = jnp.exp2(sc - mx)
        pp = jnp.exp2(sp - mx)
        den = jnp.sum(pc, axis=-1, keepdims=True) + jnp.sum(pp, axis=-1, keepdims=True)
        return (_dot(pc.astype(BF16), vc_ref[0:n_cur, :]) + _dot(pp.astype(BF16), vp_ref[lo:bq, :])) / den

    items = [(r, hh) for r in range(bq // rb) for hh in range(2)]
    pending = [stage_logits(*it) for it in items[:BAND_AHEAD]]
    out_first = None
    for i, (r, hh) in enumerate(items):
        if i + BAND_AHEAD < len(items):
            pending.append(stage_logits(*items[i + BAND_AHEAD]))
        out = stage_softmax(r, hh, pending.pop(0))
        if hh == 0:
            out_first = out
        else:
            o_ref[r * rb:(r + 1) * rb, :] = jnp.where(first_half, out_first, out).astype(o_ref.dtype)


def _band_table_kernel(g_ref, o_ref):
    rb, width = o_ref.shape
    x = jnp.broadcast_to(g_ref[...], (rb, g_ref.shape[-1]))
    y = pltpu.roll(x, 0, 1, stride=1, stride_axis=0)[:, rb:]
    i = lax.broadcasted_iota(jnp.int32, (rb, width), 0)
    j = lax.broadcasted_iota(jnp.int32, (rb, width), 1)
    first = (i // CHUNK) * CHUNK
    ok = (j >= first) & (j < first + (BAND_CHUNKS + 1) * CHUNK)
    o_ref[...] = jnp.where(ok, y * LOG2E, NEG_INF)


def _band_table(rel_bias, rb):
    band = BAND_CHUNKS * CHUNK
    width = band + rb
    rel = band + rb - np.arange(rb + width)
    g = rel_bias[:, np.clip(rel, -REL_CLIP, REL_CLIP) + REL_CLIP].astype(F32)[:, None, :]
    n_heads = rel_bias.shape[0]
    return pl.pallas_call(
        _band_table_kernel, grid=(n_heads,),
        in_specs=[pl.BlockSpec((None, 1, rb + width), lambda h: (h, 0, 0))],
        out_specs=pl.BlockSpec((None, rb, width), lambda h: (h, 0, 0)),
        out_shape=jax.ShapeDtypeStruct((n_heads, rb, width), F32),
        compiler_params=_cparams("parallel"), name="band_table",
    )(g)


def _band_prompt(q, k, v, table, batch, seq, bq, rb):
    assert bq >= BAND_CHUNKS * CHUNK and rb % CHUNK == 0
    m, gw = q.shape
    nq = seq // bq
    cur = lambda p, b, i: (b * nq + i, p)
    prev = lambda p, b, i: (b * nq + jnp.maximum(i - 1, 0), p)
    blk = lambda f: pl.BlockSpec((bq, LANE), f)
    return pl.pallas_call(
        functools.partial(_band_kernel, bq=bq, rb=rb),
        grid=(A_HEADS // 2, batch, nq),
        in_specs=[blk(cur), blk(prev), blk(cur), blk(prev), blk(cur),
                  pl.BlockSpec((2,) + table.shape[1:], lambda p, b, i: (p, 0, 0))],
        out_specs=blk(cur),
        out_shape=jax.ShapeDtypeStruct((m, gw), BF16),
        compiler_params=_cparams("parallel", "parallel", "parallel"), name="band_prompt",
    )(q, k, k, v, v, table)


def _join_rows(cache, new, total):
    pad = total - cache.shape[0] - new.shape[0]
    parts = [cache.astype(BF16), new.astype(BF16)]
    if pad:
        parts.append(jnp.zeros((pad, cache.shape[1]), BF16))
    return jnp.concatenate(parts, axis=0)


def _band_sample_kernel(q_ref, kc_ref, vc_ref, kn_ref, vn_ref, t_ref, o_ref, *, total):
    t_rows = q_ref.shape[0]
    first_half = lax.broadcasted_iota(jnp.int32, (t_rows, LANE), 1) < (LANE // 2)
    for p in range(A_HEADS // 2):
        cols = slice(p * LANE, (p + 1) * LANE)
        q = q_ref[:, cols]
        k = _join_rows(kc_ref[:, cols], kn_ref[:, cols], total)
        v = _join_rows(vc_ref[:, cols], vn_ref[:, cols], total)
        outs = []
        for hh in range(2):
            sel = first_half if hh == 0 else jnp.logical_not(first_half)
            qh = jnp.where(sel, q, jnp.zeros_like(q))
            s = t_ref[2 * p + hh] * LOG2E + _dot_nt(qh, k)
            mx = jnp.max(s, axis=-1, keepdims=True)
            pr = jnp.exp2(s - mx)
            outs.append(_dot(pr.astype(BF16), v) / jnp.sum(pr, axis=-1, keepdims=True))
        o_ref[:, cols] = jnp.where(first_half, outs[0], outs[1]).astype(o_ref.dtype)


def _mla_sample_kernel(q_ref, kc_ref, vc_ref, kn_ref, vn_ref, o_ref, *, past, total, n_heads):
    t_rows = o_ref.shape[0]
    qpos = past + lax.broadcasted_iota(jnp.int32, (t_rows, total), 0)
    kpos = lax.broadcasted_iota(jnp.int32, (t_rows, total), 1)
    ok = (kpos < past + t_rows) & (kpos // CHUNK <= qpos // CHUNK)
    for hh in range(n_heads):
        cols = slice(hh * LANE, (hh + 1) * LANE)
        k = _join_rows(kc_ref[hh], kn_ref[hh], total)
        v = _join_rows(vc_ref[:, cols], vn_ref[:, cols], total)
        s = jnp.where(ok, _dot_nt(q_ref[hh], k), NEG_INF)
        mx = jnp.max(s, axis=-1, keepdims=True)
        pr = jnp.exp2(s - mx)
        o = _dot(pr.astype(BF16), v) / jnp.sum(pr, axis=-1, keepdims=True)
        o_ref[:, cols] = o.astype(o_ref.dtype)


def _interleaved_masks(t_rows, total, past, n_heads, strict):
    qpos = past + lax.broadcasted_iota(jnp.int32, (t_rows, total), 0)
    col = lax.broadcasted_iota(jnp.int32, (t_rows, total), 1)
    kpos = col // n_heads
    causal = (kpos < qpos) if strict else (kpos <= qpos)
    return col % n_heads, causal


def _fox_sample_kernel(q_ref, kc_ref, vc_ref, kn_ref, vn_ref, fq_ref, fk_ref, o_ref, *, past, total, n_heads):
    t_rows = o_ref.shape[0]
    k = _join_rows(kc_ref[...], kn_ref[...], total)
    v = _join_rows(vc_ref[...], vn_ref[...], total)
    khead, causal = _interleaved_masks(t_rows, total, past, n_heads, strict=False)
    fk = fk_ref[...]
    for hh in range(n_heads):
        cols = slice(hh * LANE, (hh + 1) * LANE)
        s = (fq_ref[:, hh:hh + 1] + fk) * LOG2E + _dot_nt(q_ref[:, cols], k)
        s = jnp.where((khead == hh) & causal, s, NEG_INF)
        mx = jnp.max(s, axis=-1, keepdims=True)
        pr = jnp.exp2(s - mx)
        o = _dot(pr.astype(BF16), v) / jnp.sum(pr, axis=-1, keepdims=True)
        o_ref[:, cols] = o.astype(o_ref.dtype)


def _sb_sample_kernel(q_ref, kc_ref, vc_ref, kn_ref, vn_ref, o_ref, *, past, total, n_heads):
    t_rows = o_ref.shape[0]
    nb = total // LANE
    k = _join_rows(kc_ref[...], kn_ref[...], total)
    v = _join_rows(vc_ref[...], vn_ref[...], total)
    khead, causal = _interleaved_masks(t_rows, total, past, n_heads, strict=True)
    tri = _suffix_tri(LANE)
    for hh in range(n_heads):
        cols = slice(hh * LANE, (hh + 1) * LANE)
        ok = (khead == hh) & causal
        z = _dot_nt(q_ref[:, cols], k)
        log_beta = _log_sigmoid(z)
        log_fail = jnp.where(ok, log_beta - z, 0.0)
        stacked = jnp.concatenate([log_fail[:, c * LANE:(c + 1) * LANE] for c in range(nb)], axis=0)
        hi, lo = _split2(stacked)
        incl = _dot(hi, tri) + _dot(lo, tri)
        across = jnp.zeros((t_rows, 1), F32)
        w_blocks = [None] * nb
        for c in reversed(range(nb)):
            sl = slice(c * LANE, (c + 1) * LANE)
            inc = incl[c * t_rows:(c + 1) * t_rows]
            w = jnp.exp(log_beta[:, sl] + (inc - log_fail[:, sl] + across))
            w_blocks[c] = jnp.where(ok[:, sl], w, 0.0).astype(BF16)
            across = across + inc[:, 0:1]
        o_ref[:, cols] = _dot(jnp.concatenate(w_blocks, axis=-1), v).astype(o_ref.dtype)


def _sample_specs(n_streams, t_rows, past, gw):
    row = lambda b: (b, 0)
    return [pl.BlockSpec((t_rows, gw), row), pl.BlockSpec((past, gw), row), pl.BlockSpec((past, gw), row),
            pl.BlockSpec((t_rows, gw), row), pl.BlockSpec((t_rows, gw), row)]


def _pad_keys(n):
    return -(-n // LANE) * LANE


def _band_sample(q, kc, vc, kn, vn, table, n_streams):
    m, gw = q.shape
    t_rows, past = m // n_streams, kc.shape[0] // n_streams
    total = table.shape[-1]
    return pl.pallas_call(
        functools.partial(_band_sample_kernel, total=total),
        grid=(n_streams,),
        in_specs=_sample_specs(n_streams, t_rows, past, gw) + [_resident(table.shape)],
        out_specs=pl.BlockSpec((t_rows, gw), lambda b: (b, 0)),
        out_shape=jax.ShapeDtypeStruct((m, gw), BF16),
        compiler_params=_cparams("parallel"), name="band_sample",
    )(q, kc, vc, kn, vn, table)


def _interleaved_specs(t_rows, past, gw, n_heads, first_stream):
    row = lambda b: (b, 0)
    cache = pl.BlockSpec((past * n_heads, LANE), lambda b: (first_stream + b, 0))
    new = pl.BlockSpec((t_rows * n_heads, LANE), row)
    return [pl.BlockSpec((t_rows, gw), row), cache, cache, new, new]


def _fox_sample(q, kc, vc, kn, vn, f_rows, f_cols, n_streams, past, layer):
    m, gw = q.shape
    t_rows = m // n_streams
    total = f_cols.shape[-1]
    return pl.pallas_call(
        functools.partial(_fox_sample_kernel, past=past, total=total, n_heads=L_HEADS),
        grid=(n_streams,),
        in_specs=_interleaved_specs(t_rows, past, gw, L_HEADS, layer * n_streams) + [
            pl.BlockSpec((t_rows, LANE), lambda b: (b, 0)),
            pl.BlockSpec((None, 1, total), lambda b: (b, 0, 0))],
        out_specs=pl.BlockSpec((t_rows, gw), lambda b: (b, 0)),
        out_shape=jax.ShapeDtypeStruct((m, gw), BF16),
        compiler_params=_cparams("parallel"), name="fox_sample",
    )(q, kc, vc, kn, vn, f_rows, f_cols)


def _sb_sample(q, kc, vc, kn, vn, n_streams, past, layer):
    m, gw = q.shape
    t_rows = m // n_streams
    total = _pad_keys((past + t_rows) * L_HEADS)
    return pl.pallas_call(
        functools.partial(_sb_sample_kernel, past=past, total=total, n_heads=L_HEADS),
        grid=(n_streams,),
        in_specs=_interleaved_specs(t_rows, past, gw, L_HEADS, layer * n_streams),
        out_specs=pl.BlockSpec((t_rows, gw), lambda b: (b, 0)),
        out_shape=jax.ShapeDtypeStruct((m, gw), BF16),
        compiler_params=_cparams("parallel"), name="sb_sample",
    )(q, kc, vc, kn, vn)


def _mla_sample(qd, kd_c, vd_c, kd_n, vd_n, n_streams):
    n_heads, m, dk = qd.shape
    gw = vd_n.shape[1]
    t_rows, past = m // n_streams, vd_c.shape[0] // n_streams
    total = _pad_keys(past + t_rows)
    lat = lambda rows: pl.BlockSpec((n_heads, rows, dk), lambda b: (0, b, 0))
    row = lambda b: (b, 0)
    return pl.pallas_call(
        functools.partial(_mla_sample_kernel, past=past, total=total, n_heads=n_heads),
        grid=(n_streams,),
        in_specs=[lat(t_rows), lat(past), pl.BlockSpec((past, gw), row), lat(t_rows),
                  pl.BlockSpec((t_rows, gw), row)],
        out_specs=pl.BlockSpec((t_rows, gw), row),
        out_shape=jax.ShapeDtypeStruct((m, gw), BF16),
        compiler_params=_cparams("parallel"), name="mla_sample",
    )(qd, kd_c, vd_c, kd_n, vd_n)


def _out_proj_kernel(oa_ref, ob_ref, oc_ref, od_ref, gmix_ref, wo_ref, x_ref, gpost_ref, gffn_ref,
                     x1_ref, h2_ref):
    gw = oa_ref.shape[1]
    y = None
    for gi, o_ref in enumerate((oa_ref, ob_ref, oc_ref, od_ref)):
        n = _rms(o_ref[...].astype(F32), gmix_ref[gi:gi + 1, :]).astype(BF16)
        part = _dot(n, wo_ref[gi * gw:(gi + 1) * gw, :])
        y = part if y is None else y + part
    x1 = x_ref[...] + _rms(y, gpost_ref[...])
    x1_ref[...] = x1
    h2_ref[...] = _rms(x1, gffn_ref[...]).astype(BF16)


def _out_proj(outs, g_mix, w_o, x, g_post, g_ffn, tm):
    m, d = x.shape
    gw = d // 4
    rows = lambda i: (i, 0)
    return pl.pallas_call(
        _out_proj_kernel, grid=(m // tm,),
        in_specs=[pl.BlockSpec((tm, gw), rows)] * 4 + [_resident(g_mix.shape), _resident(w_o.shape),
                                                       pl.BlockSpec((tm, d), rows),
                                                       _resident((1, d)), _resident((1, d))],
        out_specs=[pl.BlockSpec((tm, d), rows), pl.BlockSpec((tm, d), rows)],
        out_shape=[jax.ShapeDtypeStruct((m, d), F32), jax.ShapeDtypeStruct((m, d), BF16)],
        compiler_params=_cparams("parallel"), name="out_proj",
    )(*outs, g_mix, w_o, x, g_post, g_ffn)


FFN_ROW_GROUP = 256


def _ffn_kernel(h_ref, wg_ref, wu_ref, wd_ref, x_ref, g_ref, o_ref, acc_ref):
    f = pl.program_id(1)

    @pl.when(f == 0)
    def _():
        acc_ref[...] = jnp.zeros_like(acc_ref)

    tm = h_ref.shape[0]
    rg = min(FFN_ROW_GROUP, tm)

    def gate_up(r):
        h = h_ref[r * rg:(r + 1) * rg, :]
        return _dot(h, wg_ref[...]), _dot(h, wu_ref[...])

    pending = [gate_up(0)]
    for r in range(tm // rg):
        if r + 1 < tm // rg:
            pending.append(gate_up(r + 1))
        gate, up = pending.pop(0)
        act = (gate * jax.nn.sigmoid(gate) * up).astype(BF16)
        acc_ref[r * rg:(r + 1) * rg, :] += _dot(act, wd_ref[...])

    @pl.when(f == pl.num_programs(1) - 1)
    def _():
        o_ref[...] = x_ref[...] + _rms(acc_ref[...], g_ref[...])


def _ffn(h2, w_gate, w_up, w_down, x1, g, tm, tf):
    m, d = x1.shape
    ff = w_gate.shape[1]
    rows = lambda i, f: (i, 0)
    return pl.pallas_call(
        _ffn_kernel, grid=(m // tm, ff // tf),
        in_specs=[pl.BlockSpec((tm, d), rows), pl.BlockSpec((d, tf), lambda i, f: (0, f)),
                  pl.BlockSpec((d, tf), lambda i, f: (0, f)), pl.BlockSpec((tf, d), lambda i, f: (f, 0)),
                  pl.BlockSpec((tm, d), rows), pl.BlockSpec((1, d), lambda i, f: (0, 0))],
        out_specs=pl.BlockSpec((tm, d), rows),
        out_shape=jax.ShapeDtypeStruct((m, d), F32),
        scratch_shapes=[pltpu.VMEM((tm, d), F32)],
        compiler_params=_cparams("parallel", "arbitrary"), name="ffn",
    )(h2, w_gate, w_up, w_down, x1, g)


def _pad_cols(w, n):
    return jnp.pad(w, ((0, 0), (0, n - w.shape[1])))


def _swap_halves(w):
    half = w.shape[1] // 2
    return jnp.concatenate([w[:, half:], w[:, :half]], axis=1)


def _layer_weights(l, d, w_in, b_f, g_q_lat, g_kv_lat, w_uq, w_uk, w_uv):
    gw = d // 4
    q_rank, kv_rank = g_q_lat.shape[1], g_kv_lat.shape[1]
    n_ab = 6 * gw
    n_in = w_in.shape[2]
    w_t = jnp.swapaxes(w_in[l], 0, 1)
    w_ab = w_t[:n_ab].astype(BF16)
    w_rest = jnp.pad(w_t[n_ab:], ((0, -(-(n_in - n_ab) // LANE) * LANE - (n_in - n_ab)), (0, 0))).astype(BF16)
    hd = NOPE_DIM + ROPE_DIM
    uq = w_uq[l]
    nope = [uq[:, h * hd:h * hd + NOPE_DIM] for h in range(L_HEADS)]
    rope = [uq[:, h * hd + NOPE_DIM:(h + 1) * hd] for h in range(L_HEADS)]
    w_q = jnp.concatenate(nope + [_pad_cols(r, LANE) for r in rope], axis=1).astype(BF16)
    w_qs = jnp.concatenate([_pad_cols(_swap_halves(r), LANE) for r in rope], axis=1).astype(BF16)
    return dict(w_ab=w_ab, w_rest=w_rest, b_f=_pad_cols(b_f[l][None, :].astype(F32), LANE), g_q=g_q_lat[l][None, :],
                g_kv=g_kv_lat[l][None, :], w_q=w_q, w_qs=w_qs,
                w_uk=w_uk[l].astype(BF16), w_uv=w_uv[l].astype(BF16))


def _rope_tables(pos):
    half = ROPE_DIM // 2
    inv_freq = ROPE_THETA ** (-jnp.arange(half, dtype=F32) / half)
    ang = pos.astype(F32)[:, None] * inv_freq[None, :]
    cos, sin = jnp.cos(ang), jnp.sin(ang)
    zero = jnp.zeros((pos.shape[0], LANE - ROPE_DIM), F32)
    return jnp.concatenate([cos, cos, zero], axis=1), jnp.concatenate([-sin, sin, zero], axis=1)


def kernel(x_prompt, x_sample, cache_a_k, cache_a_v, cache_b_k, cache_b_v, cache_b_logf, cache_c_k, cache_c_v,
           cache_d_ckv, cache_d_kpe, w_in, b_f, rel_bias, g_q_lat, g_kv_lat, w_uq, w_uk, w_uv, g_mix, w_o,
           g_pre_attn, g_post_attn, g_pre_ffn, g_post_ffn, w_gate, w_up, w_down):
    batch, seq, d = x_prompt.shape
    n_streams, t_rows, _ = x_sample.shape
    depth = w_in.shape[0]
    past = cache_b_k.shape[2]
    a_win = cache_a_k.shape[2]
    gw = d // 4
    mp, ms = batch * seq, n_streams * t_rows
    bq, rb = 512, 128
    tm_p, tm_s = 256, ms

    cos_p, sin_p = _rope_tables(jnp.arange(seq))
    cos_s, sin_s = _rope_tables(jnp.tile(past + jnp.arange(t_rows), n_streams))

    cache_ckv = cache_d_ckv.reshape(depth * n_streams * past, -1)
    cache_kpe128 = _pad_cols(cache_d_kpe.reshape(depth * n_streams * past, ROPE_DIM), LANE)
    xp = x_prompt.reshape(mp, d)
    xs = x_sample.reshape(ms, d)
    p_states, s_states = [], []
    stacked = None
    for l in range(depth):
        lw = _layer_weights(l, d, w_in, b_f, g_q_lat, g_kv_lat, w_uq, w_uk, w_uv)
        g_attn = g_pre_attn[l][None, :]
        wo = w_o[l].astype(BF16)
        wg, wu, wd = w_gate[l].astype(BF16), w_up[l].astype(BF16), w_down[l].astype(BF16)
        g_post, g_ffn, g_out = g_post_attn[l][None, :], g_pre_ffn[l][None, :], g_post_ffn[l][None, :]

        pr = _project(xp, g_attn, lw, cos_p, sin_p, tm_p, stack=(l, depth, stacked))
        stacked = {name: pr[name] for name in _STACKED_STATES}
        f_run = _cumsum(pr['logf'], seq)
        f_cols = (-f_run[:, :L_HEADS]).reshape(batch, seq, L_HEADS).transpose(0, 2, 1)
        f_cols = f_cols.reshape(batch, L_HEADS, seq // bq, bq)
        bq_flash = 2 * bq if seq % (2 * bq) == 0 else bq
        oa = _band_prompt(pr['qa'], pr['ka16'], pr['va16'], _band_table(rel_bias[l], rb), batch, seq, bq, rb)
        ob = _fox_prompt(pr['qb'], pr['kb16'], pr['vb16'], f_run, f_cols, batch, seq, bq_flash, bq, 2 * rb, 2)
        oc = _sb_prompt(pr['qc'], pr['kc16'], pr['vc16'], batch, seq, bq, rb)
        od = _mla_prompt(pr['qd'], pr['kd'], pr['vd'], batch, seq, bq_flash, bq_flash, 2 * rb, 2)
        x1, h2 = _out_proj((oa, ob, oc, od), g_mix[l], wo, xp, g_post, g_ffn, 2 * tm_p)
        xp = _ffn(h2, wg, wu, wd, x1, g_out, 512, 512)
        keep = min(a_win, seq)
        p_states.append((
            pr['ka'].reshape(batch, seq, gw)[:, seq - keep:].reshape(batch, keep, A_HEADS, gw // A_HEADS),
            pr['va'].reshape(batch, seq, gw)[:, seq - keep:].reshape(batch, keep, A_HEADS, gw // A_HEADS),
            pr['logf'][:, :L_HEADS].reshape(batch, seq, L_HEADS),
            pr['ckv'].reshape(batch, seq, -1), pr['kpe'].reshape(batch, seq, ROPE_DIM)))

        sr = _project(xs, g_attn, lw, cos_s, sin_s, tm_s)
        qa, ka, va, qb, kb, vb, qc, kc, vc = (sr[n] for n in ('qa', 'ka', 'va', 'qb', 'kb', 'vb', 'qc', 'kc', 'vc'))
        logf, qd, ckv, kpe = (sr[n] for n in ('logf', 'qd', 'ckv', 'kpe'))
        tot_a = _pad_keys(a_win + t_rows)
        rel = (a_win + np.arange(t_rows))[:, None] - np.arange(tot_a)[None, :]
        tab = rel_bias[l][:, np.clip(rel, -REL_CLIP, REL_CLIP) + REL_CLIP].astype(F32)
        tab = jnp.where(jnp.asarray(np.arange(tot_a) < a_win + t_rows)[None, None, :], tab, NEG_INF)
        oa = _band_sample(qa, cache_a_k[l].reshape(n_streams * a_win, gw),
                          cache_a_v[l].reshape(n_streams * a_win, gw), ka, va, tab, n_streams)
        f_new = _cumsum(logf, t_rows)
        cl = _pad_cols(cache_b_logf[l].reshape(n_streams * past, L_HEADS).astype(F32), LANE)
        suffix = _cumsum(cl, past, reverse=True)
        n_keys = (past + t_rows) * L_HEADS
        f_cols = jnp.concatenate([suffix[:, :L_HEADS].reshape(n_streams, past * L_HEADS),
                                  -f_new[:, :L_HEADS].reshape(n_streams, t_rows * L_HEADS),
                                  jnp.zeros((n_streams, _pad_keys(n_keys) - n_keys), F32)], axis=1)[:, None, :]
        rows_of = lambda a: a.reshape(-1, LANE)
        ob = _fox_sample(qb, rows_of(cache_b_k), rows_of(cache_b_v), rows_of(kb), rows_of(vb),
                         f_new, f_cols, n_streams, past, l)
        oc = _sb_sample(qc, rows_of(cache_c_k), rows_of(cache_c_v), rows_of(kc), rows_of(vc), n_streams, past, l)
        kd_c, vd_c = _mla_keys(cache_ckv, cache_kpe128, lw['w_uk'], lw['w_uv'], 512,
                               m=n_streams * past, first_block=l * n_streams * past // 512)
        od = _mla_sample(qd, kd_c, vd_c, sr['kd'], sr['vd'], n_streams)
        x1, h2 = _out_proj((oa, ob, oc, od), g_mix[l], wo, xs, g_post, g_ffn, tm_s)
        xs = _ffn(h2, wg, wu, wd, x1, g_out, ms, 512)
        s_states.append((
            ka.reshape(n_streams, t_rows, A_HEADS, gw // A_HEADS),
            va.reshape(n_streams, t_rows, A_HEADS, gw // A_HEADS),
            kb.reshape(n_streams, t_rows, L_HEADS, LANE), vb.reshape(n_streams, t_rows, L_HEADS, LANE),
            logf[:, :L_HEADS].reshape(n_streams, t_rows, L_HEADS),
            kc.reshape(n_streams, t_rows, L_HEADS, LANE), vc.reshape(n_streams, t_rows, L_HEADS, LANE),
            ckv.reshape(n_streams, t_rows, -1), kpe.reshape(n_streams, t_rows, ROPE_DIM)))

    p_a_k, p_a_v, p_b_logf, p_d_ckv, p_d_kpe = [jnp.stack(s) for s in zip(*p_states)]
    p_b_k, p_b_v, p_c_k, p_c_v = (stacked[name].reshape(depth, batch, seq, L_HEADS, LANE)
                                  for name in _STACKED_STATES)
    s_out = [jnp.stack(s) for s in zip(*s_states)]
    return (xp.reshape(batch, seq, d), xs.reshape(n_streams, t_rows, d),
            p_a_k, p_a_v, p_b_k, p_b_v, p_b_logf, p_c_k, p_c_v, p_d_ckv, p_d_kpe, *s_out)
```

```python
import functools
import math

import numpy as np
import jax
import jax.numpy as jnp
from jax import lax
from jax.experimental import pallas as pl
from jax.experimental.pallas import tpu as pltpu

CHUNK = 64
A_HEADS = 8
L_HEADS = 4
BAND_CHUNKS = 8
REL_CLIP = 128
NOPE_DIM = 128
ROPE_DIM = 64
ROPE_THETA = 10000.0
RMS_EPS = 1e-6
NEG_INF = -1e30
LOG2E = math.log2(math.e)

LANE = 128
MXU_DEPTH = 256
VMEM_LIMIT = 56 * 1024 * 1024

F32 = jnp.float32
BF16 = jnp.bfloat16


def _cparams(*sem):
    return pltpu.CompilerParams(dimension_semantics=sem, vmem_limit_bytes=VMEM_LIMIT)


def _resident(shape):
    return pl.BlockSpec(shape, lambda *_: (0,) * len(shape), pipeline_mode=pl.Buffered(1))


def _rms(x, g):
    return x * lax.rsqrt(jnp.mean(x * x, axis=-1, keepdims=True) + RMS_EPS) * g


def _log_sigmoid(x):
    return jnp.minimum(x, 0.0) - jnp.log1p(jnp.exp(-jnp.abs(x)))


def _dot(a, b):
    return jnp.dot(a, b, preferred_element_type=F32)


def _dot_nt(a, b):
    return lax.dot_general(a, b, (((1,), (1,)), ((), ())), preferred_element_type=F32)


def _split2(x):
    hi = x.astype(BF16)
    lo = (x - hi.astype(F32)).astype(BF16)
    return hi, lo


def _split3(x):
    hi = x.astype(BF16)
    r = x - hi.astype(F32)
    mid = r.astype(BF16)
    lo = (r - mid.astype(F32)).astype(BF16)
    return hi, mid, lo


_PROJ_INPUTS = 13


def _latent_keys(ckv, kpe, wuk_ref, wuv_ref, kd_ref, vd_ref):
    c = ckv.astype(BF16)
    kn = _dot(c, wuk_ref[...])
    vd_ref[...] = _dot(c, wuv_ref[...]).astype(BF16)
    kp = kpe.astype(BF16)
    for hh in range(kd_ref.shape[0]):
        kd_ref[hh, :, 0:LANE] = kn[:, LANE * hh:LANE * (hh + 1)].astype(BF16)
        kd_ref[hh, :, LANE:2 * LANE] = kp


def _proj_kernel(*refs, gw, q_rank, kv_rank, rope_dim, q_scales):
    (x_ref, g_ref, wab_ref, wrest_ref, bf_ref, gq_ref, gkv_ref, wq_ref, wqs_ref, wuk_ref, wuv_ref, cos_ref, sin_ref,
     ) = refs[:_PROJ_INPUTS]
    (qa_ref, ka_ref, va_ref, ka16_ref, va16_ref, qb_ref, kb_ref, vb_ref, kb16_ref, vb16_ref,
     qc_ref, kc_ref, vc_ref, kc16_ref, vc16_ref, logf_ref, qd_ref, ckv_ref, kpe_ref, kd_ref, vd_ref,
     ) = refs[len(refs) - len(_PROJ_OUTPUTS):]
    h = _rms(x_ref[...], g_ref[...]).astype(BF16)

    n_ab = wab_ref.shape[0]
    z_rest = _dot_nt(h, wrest_ref[...])
    zf = z_rest[:, 0:LANE]
    z_rest = pltpu.roll(z_rest, z_rest.shape[1] - L_HEADS, 1)

    def seg(a, n):
        if a < n_ab:
            return _dot_nt(h, wab_ref[a:a + n, :])
        return z_rest[:, a - n_ab:a - n_ab + n]

    groups = ((qa_ref, ka_ref, va_ref, ka16_ref, va16_ref), (qb_ref, kb_ref, vb_ref, kb16_ref, vb16_ref),
              (qc_ref, kc_ref, vc_ref, kc16_ref, vc16_ref))
    for gi, (q_r, k_r, v_r, k16_r, v16_r) in enumerate(groups):
        base = 3 * gw * gi
        q_r[...] = (seg(base, gw) * q_scales[gi]).astype(BF16)
        k = seg(base + gw, gw)
        k16_r[...] = k.astype(BF16)
        v = seg(base + 2 * gw, gw)
        v16_r[...] = v.astype(BF16)
        for state_ref, val in ((k_r, k), (v_r, v)):
            if state_ref.shape[-1] == gw:
                state_ref[...] = val
            else:
                n_h = gw // LANE
                for hh in range(n_h):
                    state_ref[pl.ds(hh, val.shape[0], stride=n_h), :] = val[:, hh * LANE:(hh + 1) * LANE]
    off = 9 * gw
    dq = seg(off, q_rank)
    off += q_rank
    dkv = seg(off, kv_rank)
    off += kv_rank
    dkr = seg(off, LANE)
    half = rope_dim // 2
    lane = lax.broadcasted_iota(jnp.int32, dkr.shape, 1)
    dkr_sw = jnp.where(lane < half, pltpu.roll(dkr, LANE - half, 1), pltpu.roll(dkr, half, 1))

    logf_ref[...] = _log_sigmoid(zf + bf_ref[...])
    ckv = _rms(dkv, gkv_ref[...])
    ckv_ref[...] = ckv
    cos = cos_ref[...]
    sin = sin_ref[...]
    kpe = dkr * cos + dkr_sw * sin
    kpe_ref[...] = kpe[:, :rope_dim]
    _latent_keys(ckv, kpe, wuk_ref, wuv_ref, kd_ref, vd_ref)

    nq = _rms(dq, gq_ref[...]).astype(BF16)
    qf = _dot(nq, wq_ref[...])
    qs = _dot(nq, wqs_ref[...])
    n_heads = qd_ref.shape[0]
    for hh in range(n_heads):
        qd_ref[hh, :, 0:LANE] = (qf[:, LANE * hh:LANE * (hh + 1)] * q_scales[3]).astype(BF16)
        r0 = LANE * (n_heads + hh)
        rot = qf[:, r0:r0 + LANE] * cos + qs[:, LANE * hh:LANE * (hh + 1)] * sin
        qd_ref[hh, :, LANE:2 * LANE] = (rot * q_scales[3]).astype(BF16)


def _q_scales(gw):
    return ((gw // A_HEADS) ** -0.5 * LOG2E, (gw // L_HEADS) ** -0.5 * LOG2E, (gw // L_HEADS) ** -0.5,
            (NOPE_DIM + ROPE_DIM) ** -0.5 * LOG2E)


_PROJ_OUTPUTS = ('qa', 'ka', 'va', 'ka16', 'va16', 'qb', 'kb', 'vb', 'kb16', 'vb16',
                 'qc', 'kc', 'vc', 'kc16', 'vc16', 'logf', 'qd', 'ckv', 'kpe', 'kd', 'vd')


_STACKED_STATES = ('kb', 'vb', 'kc', 'vc')


def _project(x, g, lw, cos_t, sin_t, tm, stack=None):
    m, d = x.shape
    gw = d // 4
    q_rank, kv_rank = lw['g_q'].shape[1], lw['g_kv'].shape[1]
    n_pos_blocks = cos_t.shape[0] // tm
    rows = lambda i: (i, 0)
    pos = lambda i: (i % n_pos_blocks, 0)
    f32o = lambda n: jax.ShapeDtypeStruct((m, n), F32)
    bf16o = lambda n: jax.ShapeDtypeStruct((m, n), BF16)
    blk = lambda n: pl.BlockSpec((tm, n), rows)
    group = [bf16o(gw), f32o(gw), f32o(gw), bf16o(gw), bf16o(gw)]
    per_head = jax.ShapeDtypeStruct((L_HEADS, m, 2 * LANE), BF16)
    per_head_blk = pl.BlockSpec((L_HEADS, tm, 2 * LANE), lambda i: (0, i, 0))
    out_shape = group * 3 + [f32o(LANE), per_head, f32o(kv_rank), f32o(ROPE_DIM), per_head, bf16o(gw)]
    out_specs = [blk(gw)] * 15 + [blk(LANE), per_head_blk, blk(kv_rank), blk(ROPE_DIM), per_head_blk, blk(gw)]
    in_specs = [blk(d), _resident((1, d)), _resident(lw['w_ab'].shape), _resident(lw['w_rest'].shape),
                _resident((1, LANE)),
                _resident((1, q_rank)), _resident((1, kv_rank)),
                _resident(lw['w_q'].shape), _resident(lw['w_qs'].shape),
                _resident(lw['w_uk'].shape), _resident(lw['w_uv'].shape),
                pl.BlockSpec((tm, LANE), pos), pl.BlockSpec((tm, LANE), pos)]
    args = [x, g, lw['w_ab'], lw['w_rest'], lw['b_f'], lw['g_q'], lw['g_kv'], lw['w_q'], lw['w_qs'],
            lw['w_uk'], lw['w_uv'], cos_t, sin_t]
    aliases = {}
    if stack is not None:
        layer, depth, buffers = stack
        for name in _STACKED_STATES:
            o = _PROJ_OUTPUTS.index(name)
            n_h = gw // LANE
            out_shape[o] = jax.ShapeDtypeStruct((depth, m * n_h, LANE), F32)
            out_specs[o] = pl.BlockSpec((None, tm * n_h, LANE), lambda i: (layer, i, 0))
            if buffers is not None:
                aliases[len(args)] = o
                args.append(buffers[name])
                in_specs.append(pl.BlockSpec(memory_space=pl.ANY))
    kern = functools.partial(_proj_kernel, gw=gw, q_rank=q_rank, kv_rank=kv_rank, rope_dim=ROPE_DIM,
                             q_scales=_q_scales(gw))
    outs = pl.pallas_call(
        kern, grid=(m // tm,), in_specs=in_specs, out_specs=out_specs, out_shape=out_shape,
        input_output_aliases=aliases, compiler_params=_cparams("parallel"), name="in_proj",
    )(*args)
    return dict(zip(_PROJ_OUTPUTS, outs))


def _mla_keys_kernel(ckv_ref, kpe_ref, wuk_ref, wuv_ref, kd_ref, vd_ref):
    _latent_keys(ckv_ref[...], kpe_ref[...], wuk_ref, wuv_ref, kd_ref, vd_ref)


def _mla_keys(ckv, kpe128, w_uk, w_uv, tm, m=None, first_block=0):
    r = ckv.shape[1]
    m = ckv.shape[0] if m is None else m
    n = w_uk.shape[1]
    rows = lambda i: (i, 0)
    src = lambda i: (first_block + i, 0)
    return pl.pallas_call(
        _mla_keys_kernel, grid=(m // tm,),
        in_specs=[pl.BlockSpec((tm, r), src), pl.BlockSpec((tm, LANE), src),
                  _resident(w_uk.shape), _resident(w_uv.shape)],
        out_specs=[pl.BlockSpec((L_HEADS, tm, 2 * LANE), lambda i: (0, i, 0)), pl.BlockSpec((tm, n), rows)],
        out_shape=[jax.ShapeDtypeStruct((L_HEADS, m, 2 * LANE), BF16), jax.ShapeDtypeStruct((m, n), BF16)],
        compiler_params=_cparams("parallel"), name="mla_keys",
    )(ckv, kpe128, w_uk, w_uv)


def _cumsum_kernel(x_ref, o_ref, carry_ref, *, seg, reverse):
    @pl.when(pl.program_id(1) == 0)
    def _():
        carry_ref[...] = jnp.zeros_like(carry_ref)

    x = x_ref[...]
    tc = x.shape[0]
    r = lax.broadcasted_iota(jnp.int32, (tc, tc), 0)
    c = lax.broadcasted_iota(jnp.int32, (tc, tc), 1)
    keep = (c > r) if reverse else (c <= r)
    if seg < tc:
        keep = keep & (r // seg == c // seg)
    tri = jnp.where(keep, 1.0, 0.0).astype(BF16)
    hi, mid, lo = _split3(x)
    y = _dot(tri, hi) + _dot(tri, mid) + _dot(tri, lo)
    o_ref[...] = y + carry_ref[...]
    if seg > tc:
        carry_ref[...] += jnp.sum(x, axis=0, keepdims=True)


def _cumsum(x, seg, reverse=False):
    m, n = x.shape
    tc = min(512, m) if seg >= 512 else min(256, m)
    nb = max(seg // tc, 1)
    nseg = m // (nb * tc)
    if reverse:
        idx = lambda s, j: (s * nb + nb - 1 - j, 0)
    else:
        idx = lambda s, j: (s * nb + j, 0)
    return pl.pallas_call(
        functools.partial(_cumsum_kernel, seg=seg, reverse=reverse), grid=(nseg, nb),
        in_specs=[pl.BlockSpec((tc, n), idx)], out_specs=pl.BlockSpec((tc, n), idx),
        out_shape=jax.ShapeDtypeStruct((m, n), F32),
        scratch_shapes=[pltpu.VMEM((1, n), F32)],
        compiler_params=_cparams("parallel", "arbitrary"), name="cumsum",
    )(x)


def _lane_tile(x, n):
    return x if n == LANE else jnp.concatenate([x] * (n // LANE), axis=1)


FOX_UNDERFLOW = 152.0


def _flash_kernel(*refs, fox, bq, bk, rb, ahead):
    if fox:
        q_ref, k_ref, v_ref, fk_ref, o_ref, acc_ref, m_ref, l_ref, kmax_ref = refs
    else:
        q_ref, k_ref, v_ref, o_ref, acc_ref, m_ref, l_ref = refs
    qi = pl.program_id(2)
    m_ref[...] = jnp.full_like(m_ref, NEG_INF)
    l_ref[...] = jnp.zeros_like(l_ref)
    acc_ref[...] = jnp.zeros_like(acc_ref)

    def logits(r, j):
        rows = slice(r * rb, (r + 1) * rb)
        s = _dot_nt(q_ref[rows, :], k_ref[pl.ds(pl.multiple_of(j * bk, bk), bk), :])
        if fox:
            s = fk_ref[pl.ds(j, 1), :] * LOG2E + s
        return s

    def update(r, s, j):
        rows = slice(r * rb, (r + 1) * rb)
        m_prev = m_ref[rows, :]
        m_new = jnp.maximum(m_prev, jnp.max(s, axis=-1, keepdims=True))
        alpha = jnp.exp2(m_prev - m_new)
        p = jnp.exp2(s - _lane_tile(m_new, s.shape[1]))
        l_ref[rows, :] = alpha * l_ref[rows, :] + jnp.sum(p, axis=-1, keepdims=True)
        v = v_ref[pl.ds(pl.multiple_of(j * bk, bk), bk), :]
        acc_ref[rows, :] = alpha * acc_ref[rows, :] + _dot(p.astype(BF16), v)
        m_ref[rows, :] = m_new

    def run(items):
        pending = [logits(r, j) for r, j, _ in items[:ahead]]
        for i, (r, j, diag) in enumerate(items):
            if i + ahead < len(items):
                pending.append(logits(*items[i + ahead][:2]))
            s = pending.pop(0)
            if diag is not None:
                row = r * rb + lax.broadcasted_iota(jnp.int32, s.shape, 0)
                col = diag + lax.broadcasted_iota(jnp.int32, s.shape, 1)
                ok = (col <= row) if fox else (col // CHUNK <= row // CHUNK)
                s = jnp.where(ok, s, NEG_INF)
            update(r, s, j)

    n_groups = bq // rb
    kb_per_q = bq // bk

    def full_block(j, carry):
        run([(r, j, None) for r in range(n_groups)])
        return carry

    items = []
    for kb in range(kb_per_q):
        first = kb * bk // rb
        items += [(r, qi * kb_per_q + kb, kb * bk) for r in range(first, first + bk // rb)]
        items += [(r, qi * kb_per_q + kb, None) for r in range(first + bk // rb, n_groups)]
    n_full = qi * kb_per_q
    if not fox:
        lax.fori_loop(0, n_full, full_block, 0)
        run(items)
    else:
        @pl.when(qi == 0)
        def _():
            def norm_max(c, best):
                kc = k_ref[pl.ds(pl.multiple_of(c * bk, bk), bk), :].astype(F32)
                return jnp.maximum(best, jnp.max(jnp.sum(kc * kc, axis=-1, keepdims=True)))
            kmax_ref[0] = jnp.sqrt(lax.fori_loop(0, k_ref.shape[0] // bk, norm_max, jnp.float32(0.0)))

        run(items)
        q32 = q_ref[...].astype(F32)
        row_bound = jnp.sqrt(jnp.sum(q32 * q32, axis=-1, keepdims=True)) * (kmax_ref[0] * 1.001) + 1.0

        margin = jnp.max(row_bound - m_ref[:, 0:1])
        last_key = fk_ref[...][:, bk - 1:bk] * LOG2E
        block = lax.broadcasted_iota(jnp.int32, last_key.shape, 0)
        matters = (margin + last_key >= -FOX_UNDERFLOW) & (block < n_full)
        oldest = jnp.min(jnp.where(matters, block, n_full))

        def older_block(t, carry):
            return full_block(n_full - 1 - t, carry)

        lax.fori_loop(0, n_full - oldest, older_block, 0)
    o_ref[...] = (acc_ref[...] / l_ref[...]).astype(o_ref.dtype)


def _flash_scratch(bq):
    return [pltpu.VMEM((bq, LANE), F32), pltpu.VMEM((bq, LANE), F32), pltpu.VMEM((bq, LANE), F32)]


def _fox_prompt(q, k, v, f_cols, batch, seq, bq, bk, rb, ahead):
    m, gw = q.shape
    nq = seq // bq
    qmap = lambda b, h, i: (b * nq + i, h)
    kvmap = lambda b, h, i: (b, h)
    return pl.pallas_call(
        functools.partial(_flash_kernel, fox=True, bq=bq, bk=bk, rb=rb, ahead=ahead),
        grid=(batch, L_HEADS, nq),
        in_specs=[pl.BlockSpec((bq, LANE), qmap), pl.BlockSpec((seq, LANE), kvmap),
                  pl.BlockSpec((seq, LANE), kvmap),
                  pl.BlockSpec((None, None, seq // bk, bk), lambda b, h, i: (b, h, 0, 0))],
        out_specs=pl.BlockSpec((bq, LANE), qmap),
        out_shape=jax.ShapeDtypeStruct((m, gw), BF16),
        scratch_shapes=_flash_scratch(bq) + [pltpu.SMEM((1,), F32)],
        compiler_params=_cparams("parallel", "parallel", "arbitrary"), name="fox_prompt",
    )(q, k, v, f_cols)


def _mla_prompt(qd, kd, vd, batch, seq, bq, bk, rb, ahead):
    n_heads, m, dk = qd.shape
    nq = seq // bq
    return pl.pallas_call(
        functools.partial(_flash_kernel, fox=False, bq=bq, bk=bk, rb=rb, ahead=ahead),
        grid=(batch, n_heads, nq),
        in_specs=[pl.BlockSpec((None, bq, dk), lambda b, h, i: (h, b * nq + i, 0)),
                  pl.BlockSpec((None, seq, dk), lambda b, h, i: (h, b, 0)),
                  pl.BlockSpec((seq, LANE), lambda b, h, i: (b, h))],
        out_specs=pl.BlockSpec((bq, LANE), lambda b, h, i: (b * nq + i, h)),
        out_shape=jax.ShapeDtypeStruct((m, n_heads * LANE), BF16),
        scratch_shapes=_flash_scratch(bq),
        compiler_params=_cparams("parallel", "parallel", "arbitrary"), name="mla_prompt",
    )(qd, kd, vd)


SB_AHEAD = 4
SB_UNDERFLOW = 105.0


def _suffix_tri(n):
    r = lax.broadcasted_iota(jnp.int32, (n, n), 0)
    c = lax.broadcasted_iota(jnp.int32, (n, n), 1)
    return jnp.where(r >= c, 1.0, 0.0).astype(BF16)


def _sb_kernel(q_ref, k_ref, v_ref, o_ref, acc_ref, across_ref, *, bq, rb, sub):
    qi = pl.program_id(2)
    acc_ref[...] = jnp.zeros_like(acc_ref)
    across_ref[...] = jnp.zeros_like(across_ref)
    tri = _suffix_tri(sub)

    n_groups = bq // rb

    def visit(start, diagonal):
        n_keys = [-(-((r + 1) * rb) // sub) * sub if diagonal else sub for r in range(n_groups)]

        def stage_logits(r):
            rows = slice(r * rb, (r + 1) * rb)
            return _dot_nt(q_ref[rows, :], k_ref[pl.ds(start, n_keys[r]), :])

        def stage_sums(r, z):
            log_fail = -jnp.maximum(z, 0.0) - jnp.log(1.0 + jnp.exp(-jnp.abs(z)))
            log_beta = log_fail + z
            ok = None
            if diagonal:
                row = r * rb + lax.broadcasted_iota(jnp.int32, z.shape, 0)
                col = lax.broadcasted_iota(jnp.int32, z.shape, 1)
                ok = col < row
                log_fail = jnp.where(ok, log_fail, 0.0)
            incl = {}
            for c in reversed(range(n_keys[r] // sub)):
                hi, lo = _split2(log_fail[:, c * sub:(c + 1) * sub])
                incl[c] = _dot(hi, tri) + _dot(lo, tri)
            return log_fail, log_beta, ok, incl

        def stage_values(r, state):
            log_fail, log_beta, ok, incl = state
            rows = slice(r * rb, (r + 1) * rb)
            across = across_ref[rows, :]
            acc = acc_ref[rows, :]
            for c in reversed(range(n_keys[r] // sub)):
                sl = slice(c * sub, (c + 1) * sub)
                w = jnp.exp(log_beta[:, sl] + (incl[c] - log_fail[:, sl] + _lane_tile(across, sub)))
                if diagonal:
                    w = jnp.where(ok[:, sl], w, 0.0)
                vb = v_ref[pl.ds(pl.multiple_of(start + c * sub, sub), sub), :]
                acc = acc + _dot(w.astype(BF16), vb)
                across = across + jnp.broadcast_to(incl[c][:, 0:1], across.shape)
            across_ref[rows, :] = across
            acc_ref[rows, :] = acc

        zs = [stage_logits(r) for r in range(n_groups)]
        states = [stage_sums(r, zs[r]) for r in range(min(SB_AHEAD, n_groups))]
        for r in range(n_groups):
            stage_values(r, states[r])
            if r + SB_AHEAD < n_groups:
                states.append(stage_sums(r + SB_AHEAD, zs[r + SB_AHEAD]))

    visit(pl.multiple_of(qi * bq, bq), True)

    def more(carry):
        t, worst = carry
        return jnp.logical_and(t < qi * (bq // sub), worst > -SB_UNDERFLOW)

    def earlier_block(carry):
        t, _ = carry
        visit(pl.multiple_of(qi * bq - (t + 1) * sub, sub), False)
        return t + 1, jnp.max(across_ref[...])

    lax.while_loop(more, earlier_block, (jnp.int32(0), jnp.max(across_ref[...])))
    o_ref[...] = acc_ref[...].astype(o_ref.dtype)


def _sb_prompt(q, k, v, batch, seq, bq, rb):
    m, gw = q.shape
    nq = seq // bq
    qmap = lambda b, h, i: (b * nq + i, h)
    kvmap = lambda b, h, i: (b, h)
    return pl.pallas_call(
        functools.partial(_sb_kernel, bq=bq, rb=rb, sub=MXU_DEPTH),
        grid=(batch, L_HEADS, nq),
        in_specs=[pl.BlockSpec((bq, LANE), qmap), pl.BlockSpec((seq, LANE), kvmap),
                  pl.BlockSpec((seq, LANE), kvmap)],
        out_specs=pl.BlockSpec((bq, LANE), qmap),
        out_shape=jax.ShapeDtypeStruct((m, gw), BF16),
        scratch_shapes=[pltpu.VMEM((bq, LANE), F32), pltpu.VMEM((bq, LANE), F32)],
        compiler_params=_cparams("parallel", "parallel", "arbitrary"), name="sb_prompt",
    )(q, k, v)


BAND_AHEAD = 4


def _band_kernel(q_ref, kp_ref, kc_ref, vp_ref, vc_ref, t_ref, o_ref, *, bq, rb):
    band = BAND_CHUNKS * CHUNK
    has_prev = pl.program_id(2) > 0
    first_half = lax.broadcasted_iota(jnp.int32, (rb, LANE), 1) < (LANE // 2)

    def band_cols(r):
        lo = bq - band + r * rb
        n_cur = (r + 1) * rb
        return lo, n_cur

    def stage_logits(r, hh):
        rows = slice(r * rb, (r + 1) * rb)
        lo, n_cur = band_cols(r)
        q = q_ref[rows, :]
        sel = first_half if hh == 0 else jnp.logical_not(first_half)
        qh = jnp.where(sel, q, jnp.zeros_like(q))
        return _dot_nt(qh, kc_ref[0:n_cur, :]), _dot_nt(qh, kp_ref[lo:bq, :])

    def stage_softmax(r, hh, raw):
        lo, n_cur = band_cols(r)
        n_prev = bq - lo
        sc = t_ref[hh, :, n_prev:n_prev + n_cur] + raw[0]
        sp = t_ref[hh, :, 0:n_prev] + raw[1]
        sp = jnp.where(has_prev, sp, NEG_INF)
        mx = jnp.maximum(jnp.max(sc, axis=-1, keepdims=True), jnp.max(sp, axis=-1, keepdims=True))
        pc = jnp.exp2(sc - mx)
        pp = jnp.exp2(sp - mx)
        den = jnp.sum(pc, axis=-1, keepdims=True) + jnp.sum(pp, axis=-1, keepdims=True)
        return (_dot(pc.astype(BF16), vc_ref[0:n_cur, :]) + _dot(pp.astype(BF16), vp_ref[lo:bq, :])) / den

    items = [(r, hh) for r in range(bq // rb) for hh in range(2)]
    pending = [stage_logits(*it) for it in items[:BAND_AHEAD]]
    out_first = None
    for i, (r, hh) in enumerate(items):
        if i + BAND_AHEAD < len(items):
            pending.append(stage_logits(*items[i + BAND_AHEAD]))
        out = stage_softmax(r, hh, pending.pop(0))
        if hh == 0:
            out_first = out
        else:
            o_ref[r * rb:(r + 1) * rb, :] = jnp.where(first_half, out_first, out).astype(o_ref.dtype)


def _band_table_kernel(g_ref, o_ref):
    rb, width = o_ref.shape
    x = jnp.broadcast_to(g_ref[...], (rb, g_ref.shape[-1]))
    y = pltpu.roll(x, 0, 1, stride=1, stride_axis=0)[:, rb:]
    i = lax.broadcasted_iota(jnp.int32, (rb, width), 0)
    j = lax.broadcasted_iota(jnp.int32, (rb, width), 1)
    first = (i // CHUNK) * CHUNK
    ok = (j >= first) & (j < first + (BAND_CHUNKS + 1) * CHUNK)
    o_ref[...] = jnp.where(ok, y * LOG2E, NEG_INF)


def _band_table(rel_bias, rb):
    band = BAND_CHUNKS * CHUNK
    width = band + rb
    rel = band + rb - np.arange(rb + width)
    g = rel_bias[:, np.clip(rel, -REL_CLIP, REL_CLIP) + REL_CLIP].astype(F32)[:, None, :]
    n_heads = rel_bias.shape[0]
    return pl.pallas_call(
        _band_table_kernel, grid=(n_heads,),
        in_specs=[pl.BlockSpec((None, 1, rb + width), lambda h: (h, 0, 0))],
        out_specs=pl.BlockSpec((None, rb, width), lambda h: (h, 0, 0)),
        out_shape=jax.ShapeDtypeStruct((n_heads, rb, width), F32),
        compiler_params=_cparams("parallel"), name="band_table",
    )(g)


def _band_prompt(q, k, v, table, batch, seq, bq, rb):
    assert bq >= BAND_CHUNKS * CHUNK and rb % CHUNK == 0
    m, gw = q.shape
    nq = seq // bq
    cur = lambda p, b, i: (b * nq + i, p)
    prev = lambda p, b, i: (b * nq + jnp.maximum(i - 1, 0), p)
    blk = lambda f: pl.BlockSpec((bq, LANE), f)
    return pl.pallas_call(
        functools.partial(_band_kernel, bq=bq, rb=rb),
        grid=(A_HEADS // 2, batch, nq),
        in_specs=[blk(cur), blk(prev), blk(cur), blk(prev), blk(cur),
                  pl.BlockSpec((2,) + table.shape[1:], lambda p, b, i: (p, 0, 0))],
        out_specs=blk(cur),
        out_shape=jax.ShapeDtypeStruct((m, gw), BF16),
        compiler_params=_cparams("parallel", "parallel", "parallel"), name="band_prompt",
    )(q, k, k, v, v, table)


def _join_rows(cache, new, total):
    pad = total - cache.shape[0] - new.shape[0]
    parts = [cache.astype(BF16), new.astype(BF16)]
    if pad:
        parts.append(jnp.zeros((pad, cache.shape[1]), BF16))
    return jnp.concatenate(parts, axis=0)


def _band_sample_kernel(q_ref, kc_ref, vc_ref, kn_ref, vn_ref, t_ref, o_ref, *, total):
    t_rows = q_ref.shape[0]
    first_half = lax.broadcasted_iota(jnp.int32, (t_rows, LANE), 1) < (LANE // 2)
    for p in range(A_HEADS // 2):
        cols = slice(p * LANE, (p + 1) * LANE)
        q = q_ref[:, cols]
        k = _join_rows(kc_ref[:, cols], kn_ref[:, cols], total)
        v = _join_rows(vc_ref[:, cols], vn_ref[:, cols], total)
        outs = []
        for hh in range(2):
            sel = first_half if hh == 0 else jnp.logical_not(first_half)
            qh = jnp.where(sel, q, jnp.zeros_like(q))
            s = t_ref[2 * p + hh] * LOG2E + _dot_nt(qh, k)
            mx = jnp.max(s, axis=-1, keepdims=True)
            pr = jnp.exp2(s - mx)
            outs.append(_dot(pr.astype(BF16), v) / jnp.sum(pr, axis=-1, keepdims=True))
        o_ref[:, cols] = jnp.where(first_half, outs[0], outs[1]).astype(o_ref.dtype)


def _mla_sample_kernel(q_ref, kc_ref, vc_ref, kn_ref, vn_ref, o_ref, *, past, total, n_heads):
    t_rows = o_ref.shape[0]
    qpos = past + lax.broadcasted_iota(jnp.int32, (t_rows, total), 0)
    kpos = lax.broadcasted_iota(jnp.int32, (t_rows, total), 1)
    ok = (kpos < past + t_rows) & (kpos // CHUNK <= qpos // CHUNK)
    for hh in range(n_heads):
        cols = slice(hh * LANE, (hh + 1) * LANE)
        k = _join_rows(kc_ref[hh], kn_ref[hh], total)
        v = _join_rows(vc_ref[:, cols], vn_ref[:, cols], total)
        s = jnp.where(ok, _dot_nt(q_ref[hh], k), NEG_INF)
        mx = jnp.max(s, axis=-1, keepdims=True)
        pr = jnp.exp2(s - mx)
        o = _dot(pr.astype(BF16), v) / jnp.sum(pr, axis=-1, keepdims=True)
        o_ref[:, cols] = o.astype(o_ref.dtype)


def _interleaved_masks(t_rows, total, past, n_heads, strict):
    qpos = past + lax.broadcasted_iota(jnp.int32, (t_rows, total), 0)
    col = lax.broadcasted_iota(jnp.int32, (t_rows, total), 1)
    kpos = col // n_heads
    causal = (kpos < qpos) if strict else (kpos <= qpos)
    return col % n_heads, causal


def _fox_sample_kernel(q_ref, kc_ref, vc_ref, kn_ref, vn_ref, fk_ref, o_ref, *, past, total, n_heads):
    t_rows = o_ref.shape[0]
    k = _join_rows(kc_ref[...], kn_ref[...], total)
    v = _join_rows(vc_ref[...], vn_ref[...], total)
    khead, causal = _interleaved_masks(t_rows, total, past, n_heads, strict=False)
    fk = fk_ref[...] * LOG2E
    for hh in range(n_heads):
        cols = slice(hh * LANE, (hh + 1) * LANE)
        s = fk + _dot_nt(q_ref[:, cols], k)
        s = jnp.where((khead == hh) & causal, s, NEG_INF)
        mx = jnp.max(s, axis=-1, keepdims=True)
        pr = jnp.exp2(s - mx)
        o = _dot(pr.astype(BF16), v) / jnp.sum(pr, axis=-1, keepdims=True)
        o_ref[:, cols] = o.astype(o_ref.dtype)


def _sb_sample_kernel(q_ref, kc_ref, vc_ref, kn_ref, vn_ref, o_ref, *, past, total, n_heads):
    t_rows = o_ref.shape[0]
    nb = total // LANE
    k = _join_rows(kc_ref[...], kn_ref[...], total)
    v = _join_rows(vc_ref[...], vn_ref[...], total)
    khead, causal = _interleaved_masks(t_rows, total, past, n_heads, strict=True)
    tri = _suffix_tri(LANE)
    for hh in range(n_heads):
        cols = slice(hh * LANE, (hh + 1) * LANE)
        ok = (khead == hh) & causal
        z = _dot_nt(q_ref[:, cols], k)
        log_beta = _log_sigmoid(z)
        log_fail = jnp.where(ok, log_beta - z, 0.0)
        stacked = jnp.concatenate([log_fail[:, c * LANE:(c + 1) * LANE] for c in range(nb)], axis=0)
        hi, lo = _split2(stacked)
        incl = _dot(hi, tri) + _dot(lo, tri)
        across = jnp.zeros((t_rows, 1), F32)
        w_blocks = [None] * nb
        for c in reversed(range(nb)):
            sl = slice(c * LANE, (c + 1) * LANE)
            inc = incl[c * t_rows:(c + 1) * t_rows]
            w = jnp.exp(log_beta[:, sl] + (inc - log_fail[:, sl] + across))
            w_blocks[c] = jnp.where(ok[:, sl], w, 0.0).astype(BF16)
            across = across + inc[:, 0:1]
        o_ref[:, cols] = _dot(jnp.concatenate(w_blocks, axis=-1), v).astype(o_ref.dtype)


def _sample_specs(n_streams, t_rows, past, gw):
    row = lambda b: (b, 0)
    return [pl.BlockSpec((t_rows, gw), row), pl.BlockSpec((past, gw), row), pl.BlockSpec((past, gw), row),
            pl.BlockSpec((t_rows, gw), row), pl.BlockSpec((t_rows, gw), row)]


def _pad_keys(n):
    return -(-n // LANE) * LANE


def _band_sample(q, kc, vc, kn, vn, table, n_streams):
    m, gw = q.shape
    t_rows, past = m // n_streams, kc.shape[0] // n_streams
    total = table.shape[-1]
    return pl.pallas_call(
        functools.partial(_band_sample_kernel, total=total),
        grid=(n_streams,),
        in_specs=_sample_specs(n_streams, t_rows, past, gw) + [_resident(table.shape)],
        out_specs=pl.BlockSpec((t_rows, gw), lambda b: (b, 0)),
        out_shape=jax.ShapeDtypeStruct((m, gw), BF16),
        compiler_params=_cparams("parallel"), name="band_sample",
    )(q, kc, vc, kn, vn, table)


def _interleaved_specs(t_rows, past, gw, n_heads, first_stream):
    row = lambda b: (b, 0)
    cache = pl.BlockSpec((past * n_heads, LANE), lambda b: (first_stream + b, 0))
    new = pl.BlockSpec((t_rows * n_heads, LANE), row)
    return [pl.BlockSpec((t_rows, gw), row), cache, cache, new, new]


def _fox_sample(q, kc, vc, kn, vn, f_cols, n_streams, past, layer):
    m, gw = q.shape
    t_rows = m // n_streams
    total = f_cols.shape[-1]
    return pl.pallas_call(
        functools.partial(_fox_sample_kernel, past=past, total=total, n_heads=L_HEADS),
        grid=(n_streams,),
        in_specs=_interleaved_specs(t_rows, past, gw, L_HEADS, layer * n_streams) + [
            pl.BlockSpec((None, 1, total), lambda b: (b, 0, 0))],
        out_specs=pl.BlockSpec((t_rows, gw), lambda b: (b, 0)),
        out_shape=jax.ShapeDtypeStruct((m, gw), BF16),
        compiler_params=_cparams("parallel"), name="fox_sample",
    )(q, kc, vc, kn, vn, f_cols)


def _sb_sample(q, kc, vc, kn, vn, n_streams, past, layer):
    m, gw = q.shape
    t_rows = m // n_streams
    total = _pad_keys((past + t_rows) * L_HEADS)
    return pl.pallas_call(
        functools.partial(_sb_sample_kernel, past=past, total=total, n_heads=L_HEADS),
        grid=(n_streams,),
        in_specs=_interleaved_specs(t_rows, past, gw, L_HEADS, layer * n_streams),
        out_specs=pl.BlockSpec((t_rows, gw), lambda b: (b, 0)),
        out_shape=jax.ShapeDtypeStruct((m, gw), BF16),
        compiler_params=_cparams("parallel"), name="sb_sample",
    )(q, kc, vc, kn, vn)


def _mla_sample(qd, kd_c, vd_c, kd_n, vd_n, n_streams):
    n_heads, m, dk = qd.shape
    gw = vd_n.shape[1]
    t_rows, past = m // n_streams, vd_c.shape[0] // n_streams
    total = _pad_keys(past + t_rows)
    lat = lambda rows: pl.BlockSpec((n_heads, rows, dk), lambda b: (0, b, 0))
    row = lambda b: (b, 0)
    return pl.pallas_call(
        functools.partial(_mla_sample_kernel, past=past, total=total, n_heads=n_heads),
        grid=(n_streams,),
        in_specs=[lat(t_rows), lat(past), pl.BlockSpec((past, gw), row), lat(t_rows),
                  pl.BlockSpec((t_rows, gw), row)],
        out_specs=pl.BlockSpec((t_rows, gw), row),
        out_shape=jax.ShapeDtypeStruct((m, gw), BF16),
        compiler_params=_cparams("parallel"), name="mla_sample",
    )(qd, kd_c, vd_c, kd_n, vd_n)


def _out_proj_kernel(oa_ref, ob_ref, oc_ref, od_ref, gmix_ref, wo_ref, x_ref, gpost_ref, gffn_ref,
                     x1_ref, h2_ref):
    gw = oa_ref.shape[1]
    y = None
    for gi, o_ref in enumerate((oa_ref, ob_ref, oc_ref, od_ref)):
        n = _rms(o_ref[...].astype(F32), gmix_ref[gi:gi + 1, :]).astype(BF16)
        part = _dot(n, wo_ref[gi * gw:(gi + 1) * gw, :])
        y = part if y is None else y + part
    x1 = x_ref[...] + _rms(y, gpost_ref[...])
    x1_ref[...] = x1
    h2_ref[...] = _rms(x1, gffn_ref[...]).astype(BF16)


def _out_proj(outs, g_mix, w_o, x, g_post, g_ffn, tm):
    m, d = x.shape
    gw = d // 4
    rows = lambda i: (i, 0)
    return pl.pallas_call(
        _out_proj_kernel, grid=(m // tm,),
        in_specs=[pl.BlockSpec((tm, gw), rows)] * 4 + [_resident(g_mix.shape), _resident(w_o.shape),
                                                       pl.BlockSpec((tm, d), rows),
                                                       _resident((1, d)), _resident((1, d))],
        out_specs=[pl.BlockSpec((tm, d), rows), pl.BlockSpec((tm, d), rows)],
        out_shape=[jax.ShapeDtypeStruct((m, d), F32), jax.ShapeDtypeStruct((m, d), BF16)],
        compiler_params=_cparams("parallel"), name="out_proj",
    )(*outs, g_mix, w_o, x, g_post, g_ffn)


FFN_ROW_GROUP = 256


def _ffn_kernel(h_ref, wg_ref, wu_ref, wd_ref, x_ref, g_ref, o_ref, acc_ref):
    f = pl.program_id(1)

    @pl.when(f == 0)
    def _():
        acc_ref[...] = jnp.zeros_like(acc_ref)

    tm = h_ref.shape[0]
    rg = min(FFN_ROW_GROUP, tm)

    def gate_up(r):
        h = h_ref[r * rg:(r + 1) * rg, :]
        return _dot(h, wg_ref[...]), _dot(h, wu_ref[...])

    pending = [gate_up(0)]
    for r in range(tm // rg):
        if r + 1 < tm // rg:
            pending.append(gate_up(r + 1))
        gate, up = pending.pop(0)
        act = (gate * jax.nn.sigmoid(gate) * up).astype(BF16)
        acc_ref[r * rg:(r + 1) * rg, :] += _dot(act, wd_ref[...])

    @pl.when(f == pl.num_programs(1) - 1)
    def _():
        o_ref[...] = x_ref[...] + _rms(acc_ref[...], g_ref[...])


def _ffn(h2, w_gate, w_up, w_down, x1, g, tm, tf):
    m, d = x1.shape
    ff = w_gate.shape[1]
    rows = lambda i, f: (i, 0)
    return pl.pallas_call(
        _ffn_kernel, grid=(m // tm, ff // tf),
        in_specs=[pl.BlockSpec((tm, d), rows), pl.BlockSpec((d, tf), lambda i, f: (0, f)),
                  pl.BlockSpec((d, tf), lambda i, f: (0, f)), pl.BlockSpec((tf, d), lambda i, f: (f, 0)),
                  pl.BlockSpec((tm, d), rows), pl.BlockSpec((1, d), lambda i, f: (0, 0))],
        out_specs=pl.BlockSpec((tm, d), rows),
        out_shape=jax.ShapeDtypeStruct((m, d), F32),
        scratch_shapes=[pltpu.VMEM((tm, d), F32)],
        compiler_params=_cparams("parallel", "arbitrary"), name="ffn",
    )(h2, w_gate, w_up, w_down, x1, g)


def _pad_cols(w, n):
    return jnp.pad(w, ((0, 0), (0, n - w.shape[1])))


def _swap_halves(w):
    half = w.shape[1] // 2
    return jnp.concatenate([w[:, half:], w[:, :half]], axis=1)


def _layer_weights(l, d, w_in, b_f, g_q_lat, g_kv_lat, w_uq, w_uk, w_uv):
    gw = d // 4
    q_rank, kv_rank = g_q_lat.shape[1], g_kv_lat.shape[1]
    n_ab = 6 * gw
    n_in = w_in.shape[2]
    w_t = jnp.swapaxes(w_in[l], 0, 1)
    w_ab = w_t[:n_ab].astype(BF16)
    w_rest = jnp.pad(w_t[n_ab:], ((0, -(-(n_in - n_ab) // LANE) * LANE - (n_in - n_ab)), (0, 0))).astype(BF16)
    hd = NOPE_DIM + ROPE_DIM
    uq = w_uq[l]
    nope = [uq[:, h * hd:h * hd + NOPE_DIM] for h in range(L_HEADS)]
    rope = [uq[:, h * hd + NOPE_DIM:(h + 1) * hd] for h in range(L_HEADS)]
    w_q = jnp.concatenate(nope + [_pad_cols(r, LANE) for r in rope], axis=1).astype(BF16)
    w_qs = jnp.concatenate([_pad_cols(_swap_halves(r), LANE) for r in rope], axis=1).astype(BF16)
    return dict(w_ab=w_ab, w_rest=w_rest, b_f=_pad_cols(b_f[l][None, :].astype(F32), LANE), g_q=g_q_lat[l][None, :],
                g_kv=g_kv_lat[l][None, :], w_q=w_q, w_qs=w_qs,
                w_uk=w_uk[l].astype(BF16), w_uv=w_uv[l].astype(BF16))


def _rope_tables(pos):
    half = ROPE_DIM // 2
    inv_freq = ROPE_THETA ** (-jnp.arange(half, dtype=F32) / half)
    ang = pos.astype(F32)[:, None] * inv_freq[None, :]
    cos, sin = jnp.cos(ang), jnp.sin(ang)
    zero = jnp.zeros((pos.shape[0], LANE - ROPE_DIM), F32)
    return jnp.concatenate([cos, cos, zero], axis=1), jnp.concatenate([-sin, sin, zero], axis=1)


def kernel(x_prompt, x_sample, cache_a_k, cache_a_v, cache_b_k, cache_b_v, cache_b_logf, cache_c_k, cache_c_v,
           cache_d_ckv, cache_d_kpe, w_in, b_f, rel_bias, g_q_lat, g_kv_lat, w_uq, w_uk, w_uv, g_mix, w_o,
           g_pre_attn, g_post_attn, g_pre_ffn, g_post_ffn, w_gate, w_up, w_down):
    batch, seq, d = x_prompt.shape
    n_streams, t_rows, _ = x_sample.shape
    depth = w_in.shape[0]
    past = cache_b_k.shape[2]
    a_win = cache_a_k.shape[2]
    gw = d // 4
    mp, ms = batch * seq, n_streams * t_rows
    bq, rb = 512, 128
    tm_p, tm_s = 256, ms

    cos_p, sin_p = _rope_tables(jnp.arange(seq))
    cos_s, sin_s = _rope_tables(jnp.tile(past + jnp.arange(t_rows), n_streams))

    cache_ckv = cache_d_ckv.reshape(depth * n_streams * past, -1)
    cache_kpe128 = _pad_cols(cache_d_kpe.reshape(depth * n_streams * past, ROPE_DIM), LANE)
    xp = x_prompt.reshape(mp, d)
    xs = x_sample.reshape(ms, d)
    p_states, s_states = [], []
    stacked = None
    for l in range(depth):
        lw = _layer_weights(l, d, w_in, b_f, g_q_lat, g_kv_lat, w_uq, w_uk, w_uv)
        g_attn = g_pre_attn[l][None, :]
        wo = w_o[l].astype(BF16)
        wg, wu, wd = w_gate[l].astype(BF16), w_up[l].astype(BF16), w_down[l].astype(BF16)
        g_post, g_ffn, g_out = g_post_attn[l][None, :], g_pre_ffn[l][None, :], g_post_ffn[l][None, :]

        pr = _project(xp, g_attn, lw, cos_p, sin_p, tm_p, stack=(l, depth, stacked))
        stacked = {name: pr[name] for name in _STACKED_STATES}
        f_run = _cumsum(pr['logf'], seq)
        f_cols = (-f_run[:, :L_HEADS]).reshape(batch, seq, L_HEADS).transpose(0, 2, 1)
        f_cols = f_cols.reshape(batch, L_HEADS, seq // bq, bq)
        bq_flash = 2 * bq if seq % (2 * bq) == 0 else bq
        oa = _band_prompt(pr['qa'], pr['ka16'], pr['va16'], _band_table(rel_bias[l], rb), batch, seq, bq, rb)
        ob = _fox_prompt(pr['qb'], pr['kb16'], pr['vb16'], f_cols, batch, seq, bq_flash, bq, 2 * rb, 2)
        oc = _sb_prompt(pr['qc'], pr['kc16'], pr['vc16'], batch, seq, bq, rb)
        od = _mla_prompt(pr['qd'], pr['kd'], pr['vd'], batch, seq, bq_flash, bq_flash, 2 * rb, 2)
        x1, h2 = _out_proj((oa, ob, oc, od), g_mix[l], wo, xp, g_post, g_ffn, 2 * tm_p)
        xp = _ffn(h2, wg, wu, wd, x1, g_out, 512, 512)
        keep = min(a_win, seq)
        p_states.append((
            pr['ka'].reshape(batch, seq, gw)[:, seq - keep:].reshape(batch, keep, A_HEADS, gw // A_HEADS),
            pr['va'].reshape(batch, seq, gw)[:, seq - keep:].reshape(batch, keep, A_HEADS, gw // A_HEADS),
            pr['logf'][:, :L_HEADS].reshape(batch, seq, L_HEADS),
            pr['ckv'].reshape(batch, seq, -1), pr['kpe'].reshape(batch, seq, ROPE_DIM)))

        sr = _project(xs, g_attn, lw, cos_s, sin_s, tm_s)
        qa, ka, va, qb, kb, vb, qc, kc, vc = (sr[n] for n in ('qa', 'ka', 'va', 'qb', 'kb', 'vb', 'qc', 'kc', 'vc'))
        logf, qd, ckv, kpe = (sr[n] for n in ('logf', 'qd', 'ckv', 'kpe'))
        tot_a = _pad_keys(a_win + t_rows)
        rel = (a_win + np.arange(t_rows))[:, None] - np.arange(tot_a)[None, :]
        tab = rel_bias[l][:, np.clip(rel, -REL_CLIP, REL_CLIP) + REL_CLIP].astype(F32)
        tab = jnp.where(jnp.asarray(np.arange(tot_a) < a_win + t_rows)[None, None, :], tab, NEG_INF)
        oa = _band_sample(qa, cache_a_k[l].reshape(n_streams * a_win, gw),
                          cache_a_v[l].reshape(n_streams * a_win, gw), ka, va, tab, n_streams)
        f_new = _cumsum(logf, t_rows)
        cl = _pad_cols(cache_b_logf[l].reshape(n_streams * past, L_HEADS).astype(F32), LANE)
        suffix = _cumsum(cl, past, reverse=True)
        n_keys = (past + t_rows) * L_HEADS
        f_cols = jnp.concatenate([suffix[:, :L_HEADS].reshape(n_streams, past * L_HEADS),
                                  -f_new[:, :L_HEADS].reshape(n_streams, t_rows * L_HEADS),
                                  jnp.zeros((n_streams, _pad_keys(n_keys) - n_keys), F32)], axis=1)[:, None, :]
        rows_of = lambda a: a.reshape(-1, LANE)
        ob = _fox_sample(qb, rows_of(cache_b_k), rows_of(cache_b_v), rows_of(kb), rows_of(vb),
                         f_cols, n_streams, past, l)
        oc = _sb_sample(qc, rows_of(cache_c_k), rows_of(cache_c_v), rows_of(kc), rows_of(vc), n_streams, past, l)
        kd_c, vd_c = _mla_keys(cache_ckv, cache_kpe128, lw['w_uk'], lw['w_uv'], 512,
                               m=n_streams * past, first_block=l * n_streams * past // 512)
        od = _mla_sample(qd, kd_c, vd_c, sr['kd'], sr['vd'], n_streams)
        x1, h2 = _out_proj((oa, ob, oc, od), g_mix[l], wo, xs, g_post, g_ffn, tm_s)
        xs = _ffn(h2, wg, wu, wd, x1, g_out, ms, 512)
        s_states.append((
            ka.reshape(n_streams, t_rows, A_HEADS, gw // A_HEADS),
            va.reshape(n_streams, t_rows, A_HEADS, gw // A_HEADS),
            kb.reshape(n_streams, t_rows, L_HEADS, LANE), vb.reshape(n_streams, t_rows, L_HEADS, LANE),
            logf[:, :L_HEADS].reshape(n_streams, t_rows, L_HEADS),
            kc.reshape(n_streams, t_rows, L_HEADS, LANE), vc.reshape(n_streams, t_rows, L_HEADS, LANE),
            ckv.reshape(n_streams, t_rows, -1), kpe.reshape(n_streams, t_rows, ROPE_DIM)))

    p_a_k, p_a_v, p_b_logf, p_d_ckv, p_d_kpe = [jnp.stack(s) for s in zip(*p_states)]
    p_b_k, p_b_v, p_c_k, p_c_v = (stacked[name].reshape(depth, batch, seq, L_HEADS, LANE)
                                  for name in _STACKED_STATES)
    s_out = [jnp.stack(s) for s in zip(*s_states)]
    return (xp.reshape(batch, seq, d), xs.reshape(n_streams, t_rows, d),
            p_a_k, p_a_v, p_b_k, p_b_v, p_b_logf, p_c_k, p_c_v, p_d_ckv, p_d_kpe, *s_out)
```
